```python
import jax, jax.numpy as jnp
from jax import lax
import numpy as np

D_MODEL = 1024
BATCH = 2
SEQ = 8192
DEPTH = 4

ATTN_HEADS = 16
HEAD_DIM = 64
ATTN_WIDTH = ATTN_HEADS * HEAD_DIM
IDX_HEADS = 8
IDX_DIM = 64
IDX_SCALE = (IDX_HEADS ** -0.5) * (IDX_DIM ** -0.5)
TOPK_MAX = 256
Q_BLOCK = 128
ROPE_THETA = 10000.0
D_INNER = 2 * D_MODEL
SSM_HEAD_DIM = 64
SSM_HEADS = D_INNER // SSM_HEAD_DIM
SSM_GROUPS = 4
HEADS_PER_GROUP = SSM_HEADS // SSM_GROUPS
D_STATE = 128
SSM_CONV = 4
SSM_CHUNK = 256
XBC_WIDTH = D_INNER + 2 * SSM_GROUPS * D_STATE
D_FF = 2816
FFN_CONV = 3
EPS = 1e-6
SPLITS = (ATTN_WIDTH, ATTN_WIDTH, ATTN_WIDTH, IDX_HEADS * IDX_DIM, IDX_DIM, IDX_HEADS,
          D_INNER, XBC_WIDTH, SSM_HEADS, D_MODEL, D_MODEL)
IN_WIDTH = sum(SPLITS)
SPLIT_POINTS = tuple(int(v) for v in np.cumsum(SPLITS)[:-1])

kernel_name = "hybrid_dsa_ssd_convffn_adaln"


def rms_norm(x, g):
    xf = x.astype(jnp.float32)
    y = xf * lax.rsqrt(jnp.mean(xf * xf, axis=-1, keepdims=True) + EPS)
    return (y * g.astype(jnp.float32)).astype(x.dtype)


def rope_tables(positions, dim):
    inv = 1.0 / (ROPE_THETA ** (jnp.arange(0, dim, 2, dtype=jnp.float32) / dim))
    ang = positions.astype(jnp.float32)[..., None] * inv
    return jnp.cos(ang), jnp.sin(ang)


def apply_rope(x, cos, sin):
    xf = x.astype(jnp.float32)
    x1, x2 = jnp.split(xf, 2, axis=-1)
    c = cos[:, :, None, :]
    s = sin[:, :, None, :]
    return jnp.concatenate([x1 * c - x2 * s, x2 * c + x1 * s], axis=-1).astype(x.dtype)


def causal_dwconv(x, w, b):
    width = w.shape[0]
    y = lax.conv_general_dilated(
        x, w[:, None, :].astype(x.dtype), window_strides=(1,), padding=[(width - 1, 0)],
        dimension_numbers=("NWC", "WIO", "NWC"), feature_group_count=x.shape[-1])
    return y + b.astype(x.dtype)


def dsa_attention(q, k, v, iq, ik, iw, n_keep):
    bsz, seq = q.shape[:2]
    n_blocks = seq // Q_BLOCK
    kpos = jnp.arange(seq)
    scale = HEAD_DIM ** -0.5

    def block(i):
        start = i * Q_BLOCK
        qb = lax.dynamic_slice_in_dim(q, start, Q_BLOCK, axis=1)
        iqb = lax.dynamic_slice_in_dim(iq, start, Q_BLOCK, axis=1)
        iwb = lax.dynamic_slice_in_dim(iw, start, Q_BLOCK, axis=1)
        qpos = start + jnp.arange(Q_BLOCK)
        causal = kpos[None, :] <= qpos[:, None]
        logits = jnp.einsum("bqhd,bsd->bqhs", iqb, ik)
        score = jnp.einsum("bqh,bqhs->bqs", iwb, jax.nn.relu(logits)).astype(jnp.float32) * IDX_SCALE
        score = jnp.where(causal[None], score, -jnp.inf)
        _, sel = lax.top_k(score, n_keep)
        kg = jax.vmap(lambda kb, ib: kb[ib])(k, sel)
        vg = jax.vmap(lambda vb, ib: vb[ib])(v, sel)
        s = jnp.einsum("bqhd,bqkhd->bhqk", qb, kg).astype(jnp.float32) * scale
        valid = sel <= qpos[None, :, None]
        s = jnp.where(valid[:, None], s, -jnp.inf)
        p = jax.nn.softmax(s, axis=-1).astype(v.dtype)
        return jnp.einsum("bhqk,bqkhd->bqhd", p, vg)

    out = lax.map(block, jnp.arange(n_blocks))
    return out.transpose(1, 0, 2, 3, 4).reshape(bsz, seq, ATTN_WIDTH)


def segsum(a):
    t = a.shape[-1]
    cs = jnp.cumsum(a, axis=-1)
    seg = cs[..., :, None] - cs[..., None, :]
    mask = jnp.tril(jnp.ones((t, t), dtype=bool))
    return jnp.where(mask, seg, -jnp.inf)


def ssd_scan(xs, a, bm, cm):
    bsz, seq = xs.shape[:2]
    t = SSM_CHUNK
    pad = (-seq) % t
    if pad:
        padw = lambda z: jnp.pad(z, [(0, 0), (0, pad)] + [(0, 0)] * (z.ndim - 2))
        xs, a, bm, cm = padw(xs), padw(a), padw(bm), padw(cm)
    nc = (seq + pad) // t
    xs_c = xs.reshape(bsz, nc, t, SSM_GROUPS, HEADS_PER_GROUP, SSM_HEAD_DIM)
    bm_c = bm.reshape(bsz, nc, t, SSM_GROUPS, D_STATE)
    cm_c = cm.reshape(bsz, nc, t, SSM_GROUPS, D_STATE)
    a_c = a.reshape(bsz, nc, t, SSM_GROUPS, HEADS_PER_GROUP).transpose(0, 3, 4, 1, 2)
    a_cs = jnp.cumsum(a_c, axis=-1)
    decay = jnp.exp(segsum(a_c))
    cb = jnp.einsum("bclgn,bcsgn->bgcls", cm_c, bm_c)
    y_diag = jnp.einsum("bgcls,bgrcls,bcsgrp->bclgrp", cb, decay, xs_c)
    decay_states = jnp.exp(a_cs[..., -1:] - a_cs)
    states = jnp.einsum("bclgn,bgrcl,bclgrp->bcgrpn", bm_c, decay_states, xs_c)
    states = jnp.concatenate([jnp.zeros_like(states[:, :1]), states], axis=1)
    chunk_decay = jnp.exp(segsum(jnp.pad(a_cs[..., -1], ((0, 0), (0, 0), (0, 0), (1, 0)))))
    prev_states = jnp.einsum("bgrzc,bcgrpn->bzgrpn", chunk_decay, states)[:, :-1]
    y_off = jnp.einsum("bclgn,bcgrpn,bgrcl->bclgrp", cm_c, prev_states, jnp.exp(a_cs))
    y = (y_diag + y_off).reshape(bsz, nc * t, SSM_GROUPS, HEADS_PER_GROUP, SSM_HEAD_DIM)[:, :seq]
    return y.astype(xs.dtype)


def setup_inputs(seed: int = 0) -> dict:
    key = jax.random.key(seed)
    ks = jax.random.split(key, 28)
    f32 = jnp.float32

    def nrm(k, shape, fan_in, gain=1.0):
        return jax.random.normal(k, shape, f32) * (gain * fan_in ** -0.5)

    def gain_vec(k, shape):
        return 1.0 + 0.1 * jax.random.normal(k, shape, f32)

    x = jax.random.normal(ks[0], (BATCH, SEQ, D_MODEL), f32)
    c = jax.random.normal(ks[1], (BATCH, D_MODEL), f32)
    start = jax.random.randint(ks[2], (BATCH, 1), 0, 4096, dtype=jnp.int32)
    positions = start + jnp.arange(SEQ, dtype=jnp.int32)[None, :]
    dt0 = jnp.exp(jax.random.uniform(ks[3], (DEPTH, SSM_HEADS), f32, np.log(1e-3), np.log(1e-1)))
    dt_bias = dt0 + jnp.log(-jnp.expm1(-dt0))
    a_log = jnp.log(jax.random.uniform(ks[4], (DEPTH, SSM_HEADS), f32, 1.0, 16.0))
    return {
        "x": x,
        "c": c,
        "positions": positions,
        "w_ada": nrm(ks[5], (DEPTH, D_MODEL, 6 * D_MODEL), D_MODEL, 0.5),
        "b_ada": 0.01 * jax.random.normal(ks[6], (DEPTH, 6 * D_MODEL), f32),
        "norm1_g": gain_vec(ks[7], (DEPTH, D_MODEL)),
        "w_in": nrm(ks[8], (DEPTH, D_MODEL, IN_WIDTH), D_MODEL),
        "q_norm_g": gain_vec(ks[9], (DEPTH, HEAD_DIM)),
        "k_norm_g": gain_vec(ks[10], (DEPTH, HEAD_DIM)),
        "ssm_conv_w": nrm(ks[11], (DEPTH, SSM_CONV, XBC_WIDTH), SSM_CONV),
        "ssm_conv_b": 0.01 * jax.random.normal(ks[12], (DEPTH, XBC_WIDTH), f32),
        "dt_bias": dt_bias,
        "a_log": a_log,
        "d_skip": gain_vec(ks[13], (DEPTH, SSM_HEADS)),
        "ssm_norm_g": gain_vec(ks[14], (DEPTH, D_INNER)),
        "w_attn_o": nrm(ks[15], (DEPTH, ATTN_WIDTH, D_MODEL), ATTN_WIDTH),
        "w_ssm_o": nrm(ks[16], (DEPTH, D_INNER, D_MODEL), D_INNER),
        "w_out": nrm(ks[17], (DEPTH, D_MODEL, D_MODEL), D_MODEL),
        "norm2_g": gain_vec(ks[18], (DEPTH, D_MODEL)),
        "w_up": nrm(ks[19], (DEPTH, D_MODEL, 2 * D_FF), D_MODEL),
        "ffn_conv_w": nrm(ks[20], (DEPTH, FFN_CONV, 2 * D_FF), FFN_CONV),
        "ffn_conv_b": 0.01 * jax.random.normal(ks[21], (DEPTH, 2 * D_FF), f32),
        "w_down": nrm(ks[22], (DEPTH, D_FF, D_MODEL), D_FF),
    }


def reference(x, c, positions, w_ada, b_ada, norm1_g, w_in, q_norm_g, k_norm_g, ssm_conv_w, ssm_conv_b,
              dt_bias, a_log, d_skip, ssm_norm_g, w_attn_o, w_ssm_o, w_out, norm2_g, w_up, ffn_conv_w,
              ffn_conv_b, w_down):
    bsz, seq, _ = x.shape
    n_keep = min(TOPK_MAX, seq // 4)
    cos, sin = rope_tables(positions, HEAD_DIM)
    c_act = jax.nn.silu(c)

    for l in range(DEPTH):
        mod = c_act @ w_ada[l] + b_ada[l]
        sh1, sc1, g1, sh2, sc2, g2 = [m[:, None, :] for m in jnp.split(mod, 6, axis=-1)]

        h = rms_norm(x, norm1_g[l]) * (1.0 + sc1) + sh1
        proj = h @ w_in[l]
        q, k, v, iq, ik, iw, z, xbc, dt_raw, ga, gm = jnp.split(proj, SPLIT_POINTS, axis=-1)

        q = apply_rope(rms_norm(q.reshape(bsz, seq, ATTN_HEADS, HEAD_DIM), q_norm_g[l]), cos, sin)
        k = apply_rope(rms_norm(k.reshape(bsz, seq, ATTN_HEADS, HEAD_DIM), k_norm_g[l]), cos, sin)
        v = v.reshape(bsz, seq, ATTN_HEADS, HEAD_DIM)
        iq = apply_rope(iq.reshape(bsz, seq, IDX_HEADS, IDX_DIM), cos, sin)
        ik = apply_rope(ik[:, :, None, :], cos, sin)[:, :, 0, :]
        o_attn = dsa_attention(q, k, v, iq, ik, iw, n_keep) @ w_attn_o[l]

        xbc = jax.nn.silu(causal_dwconv(xbc, ssm_conv_w[l], ssm_conv_b[l]))
        xs, bm, cm = jnp.split(xbc, [D_INNER, D_INNER + SSM_GROUPS * D_STATE], axis=-1)
        xs = xs.reshape(bsz, seq, SSM_GROUPS, HEADS_PER_GROUP, SSM_HEAD_DIM)
        bm = bm.reshape(bsz, seq, SSM_GROUPS, D_STATE)
        cm = cm.reshape(bsz, seq, SSM_GROUPS, D_STATE)
        dt = jax.nn.softplus(dt_raw.astype(jnp.float32) + dt_bias[l].astype(jnp.float32))
        a_cont = -jnp.exp(a_log[l].astype(jnp.float32))
        a_dt = (dt * a_cont).reshape(bsz, seq, SSM_GROUPS, HEADS_PER_GROUP)
        dt_g = dt.reshape(bsz, seq, SSM_GROUPS, HEADS_PER_GROUP, 1).astype(xs.dtype)
        y = ssd_scan(xs * dt_g, a_dt, bm, cm)
        y = y + xs * d_skip[l].reshape(SSM_GROUPS, HEADS_PER_GROUP, 1)
        y = y.reshape(bsz, seq, D_INNER) * jax.nn.silu(z)
        y = rms_norm(y.reshape(bsz, seq, SSM_GROUPS, D_INNER // SSM_GROUPS),
                     ssm_norm_g[l].reshape(SSM_GROUPS, D_INNER // SSM_GROUPS)).reshape(bsz, seq, D_INNER)
        o_ssm = y @ w_ssm_o[l]

        mixed = (jax.nn.sigmoid(ga) * o_attn + jax.nn.sigmoid(gm) * o_ssm) @ w_out[l]
        x = x + g1 * mixed

        h = rms_norm(x, norm2_g[l]) * (1.0 + sc2) + sh2
        u = causal_dwconv(h @ w_up[l], ffn_conv_w[l], ffn_conv_b[l])
        val, gate = jnp.split(u, 2, axis=-1)
        x = x + g2 * ((jax.nn.silu(gate) * val) @ w_down[l])

    return x
```

```python
import functools

import numpy as np
import jax
import jax.numpy as jnp
from jax import lax
from jax.experimental import pallas as pl
from jax.experimental.pallas import tpu as pltpu

F32 = jnp.float32
BF16 = jnp.bfloat16

D_MODEL = 1024
ATTN_HEADS = 16
HEAD_DIM = 64
ATTN_WIDTH = ATTN_HEADS * HEAD_DIM
IDX_HEADS = 8
IDX_DIM = 64
IDX_SCALE = (IDX_HEADS ** -0.5) * (IDX_DIM ** -0.5)
TOPK_MAX = 256
ROPE_THETA = 10000.0
D_INNER = 2 * D_MODEL
SSM_HEAD_DIM = 64
SSM_HEADS = D_INNER // SSM_HEAD_DIM
SSM_GROUPS = 4
HEADS_PER_GROUP = SSM_HEADS // SSM_GROUPS
D_STATE = 128
SSM_CONV = 4
SSM_CHUNK = 256
BC_WIDTH = 2 * SSM_GROUPS * D_STATE
D_FF = 2816
FFN_CONV = 3
EPS = 1e-6

LANES = 128
SUBLANES = 8
VMEM_LIMIT = 56 * 1024 * 1024

A_WIDTH = 3 * ATTN_WIDTH + IDX_HEADS * IDX_DIM + LANES
B_WIDTH = 2 * D_INNER + BC_WIDTH + LANES
C_WIDTH = 2 * D_MODEL

NEG_BIG = -1e30
INT_MIN = -2 ** 31
KEY_NEG_INF = -0x7F800000


def _cparams(n_axes):
    return pltpu.CompilerParams(dimension_semantics=("arbitrary",) * n_axes,
                                vmem_limit_bytes=VMEM_LIMIT)


def _sigmoid(x):
    return 1.0 / (1.0 + jnp.exp(-x))


def _silu(x):
    return x * _sigmoid(x)


def _dot(a, b):
    return jnp.dot(a, b, preferred_element_type=F32)


def _dot_nt(a, b):
    return lax.dot_general(a, b, (((1,), (1,)), ((), ())), preferred_element_type=F32)


def _split3(a):
    hi = a.astype(BF16)
    r1 = a - hi.astype(F32)
    mid = r1.astype(BF16)
    lo = (r1 - mid.astype(F32)).astype(BF16)
    return hi, mid, lo


def _dot3(a, b01):
    hi, mid, lo = _split3(a)
    return _dot(hi, b01) + _dot(mid, b01) + _dot(lo, b01)


def _dot3_left(b01, a):
    hi, mid, lo = _split3(a)
    return _dot(b01, hi) + _dot(b01, mid) + _dot(b01, lo)


def _adaln_kernel(c_ref, w_ref, b_ref, o_ref):
    c = c_ref[...]
    act = _silu(c).astype(BF16)
    o_ref[0] = _dot(act, w_ref[0].astype(BF16)) + b_ref[0]


def _adaln(c_pad, w_ada, b_ada):
    depth, d, n = w_ada.shape
    tn = 1536
    return pl.pallas_call(
        _adaln_kernel,
        grid=(depth, n // tn),
        in_specs=[pl.BlockSpec((SUBLANES, d), lambda l, j: (0, 0)),
                  pl.BlockSpec((1, d, tn), lambda l, j: (l, 0, j)),
                  pl.BlockSpec((1, 1, tn), lambda l, j: (l, 0, j))],
        out_specs=pl.BlockSpec((1, SUBLANES, tn), lambda l, j: (l, 0, j)),
        out_shape=jax.ShapeDtypeStruct((depth, SUBLANES, n), F32),
        compiler_params=_cparams(2),
    )(c_pad, w_ada, b_ada.reshape(depth, 1, n))


def _norm_mod(x, g, sc, sh):
    ms = jnp.mean(x * x, axis=-1, keepdims=True)
    h = x * lax.rsqrt(ms + EPS) * g
    return h * (1.0 + sc) + sh


def _nm_matmul_kernel(x_ref, g_ref, sc_ref, sh_ref, w_ref, o_ref, *, chunks):
    h = _norm_mod(x_ref[...], g_ref[...], sc_ref[0], sh_ref[0]).astype(BF16)
    for (c0, c1) in chunks:
        o_ref[:, c0:c1] = _dot(h, w_ref[:, c0:c1]).astype(o_ref.dtype)


def _col_chunks(n, step=1024):
    out, c = [], 0
    while c < n:
        out.append((c, min(c + step, n)))
        c += step
    return tuple(out)


def _nm_matmul(x2, g, sc, sh, w, seq, tm=256):
    m, d = x2.shape
    n = w.shape[1]
    tpb = seq // tm
    return pl.pallas_call(
        functools.partial(_nm_matmul_kernel, chunks=_col_chunks(n)),
        grid=(m // tm,),
        in_specs=[pl.BlockSpec((tm, d), lambda i: (i, 0)),
                  pl.BlockSpec((1, d), lambda i: (0, 0)),
                  pl.BlockSpec((1, 1, d), lambda i: (i // tpb, 0, 0)),
                  pl.BlockSpec((1, 1, d), lambda i: (i // tpb, 0, 0)),
                  pl.BlockSpec((d, n), lambda i: (0, 0))],
        out_specs=pl.BlockSpec((tm, n), lambda i: (i, 0)),
        out_shape=jax.ShapeDtypeStruct((m, n), F32),
        compiler_params=_cparams(1),
    )(x2, g, sc, sh, w)


def _rope64(xh, cos, sins):
    rolled = jnp.concatenate([xh[:, HEAD_DIM // 2:], xh[:, :HEAD_DIM // 2]], axis=1)
    return xh * cos + rolled * sins


def _prep_kernel(q_ref, k_ref, v_ref, iq_ref, misc_ref, cos_ref, sin_ref, qg_ref, kg_ref,
                 qo_ref, ko_ref, vo_ref, iqo_ref, iko_ref):
    cos = cos_ref[...]
    sins = sin_ref[...]

    def normed(xh, g):
        ms = jnp.mean(xh * xh, axis=-1, keepdims=True)
        return xh * lax.rsqrt(ms + EPS) * g

    qg = qg_ref[...]
    kg = kg_ref[...]
    for h in range(ATTN_HEADS):
        sl = slice(h * HEAD_DIM, (h + 1) * HEAD_DIM)
        qh = _rope64(normed(q_ref[:, sl], qg), cos, sins) * (HEAD_DIM ** -0.5)
        qo_ref[0, h] = qh.astype(BF16)
        ko_ref[0, h] = _rope64(normed(k_ref[:, sl], kg), cos, sins).astype(BF16)
        vo_ref[0, h] = v_ref[:, sl].astype(BF16)
    for h in range(IDX_HEADS):
        sl = slice(h * IDX_DIM, (h + 1) * IDX_DIM)
        iqo_ref[0, h] = _rope64(iq_ref[:, sl], cos, sins).astype(BF16)
    iko_ref[0] = _rope64(misc_ref[:, IDX_DIM:2 * IDX_DIM], cos, sins).astype(BF16)


def _prep(a_proj, cos64, sin64, qg, kg, bsz, seq, tm=256):
    tpb = seq // tm
    row = lambda b, i: b * tpb + i
    aw = ATTN_WIDTH
    iqw = IDX_HEADS * IDX_DIM
    return pl.pallas_call(
        _prep_kernel,
        grid=(bsz, tpb),
        in_specs=[pl.BlockSpec((tm, aw), lambda b, i: (row(b, i), 0)),
                  pl.BlockSpec((tm, aw), lambda b, i: (row(b, i), 1)),
                  pl.BlockSpec((tm, aw), lambda b, i: (row(b, i), 2)),
                  pl.BlockSpec((tm, iqw), lambda b, i: (row(b, i), 3 * aw // iqw)),
                  pl.BlockSpec((tm, LANES), lambda b, i: (row(b, i), (3 * aw + iqw) // LANES)),
                  pl.BlockSpec((tm, HEAD_DIM), lambda b, i: (row(b, i), 0)),
                  pl.BlockSpec((tm, HEAD_DIM), lambda b, i: (row(b, i), 0)),
                  pl.BlockSpec((1, HEAD_DIM), lambda b, i: (0, 0)),
                  pl.BlockSpec((1, HEAD_DIM), lambda b, i: (0, 0))],
        out_specs=[pl.BlockSpec((1, ATTN_HEADS, tm, HEAD_DIM), lambda b, i: (b, 0, i, 0)),
                   pl.BlockSpec((1, ATTN_HEADS, tm, HEAD_DIM), lambda b, i: (b, 0, i, 0)),
                   pl.BlockSpec((1, ATTN_HEADS, tm, HEAD_DIM), lambda b, i: (b, 0, i, 0)),
                   pl.BlockSpec((1, IDX_HEADS, tm, IDX_DIM), lambda b, i: (b, 0, i, 0)),
                   pl.BlockSpec((1, tm, IDX_DIM), lambda b, i: (b, i, 0))],
        out_shape=[jax.ShapeDtypeStruct((bsz, ATTN_HEADS, seq, HEAD_DIM), BF16),
                   jax.ShapeDtypeStruct((bsz, ATTN_HEADS, seq, HEAD_DIM), BF16),
                   jax.ShapeDtypeStruct((bsz, ATTN_HEADS, seq, HEAD_DIM), BF16),
                   jax.ShapeDtypeStruct((bsz, IDX_HEADS, seq, IDX_DIM), BF16),
                   jax.ShapeDtypeStruct((bsz, seq, IDX_DIM), BF16)],
        compiler_params=_cparams(2),
    )(a_proj, a_proj, a_proj, a_proj, a_proj, cos64, sin64, qg, kg)


def _select_kernel(iq_ref, ik_ref, misc_ref, mask_ref, key_ref, *, tq, ck, n_keep, seq):
    i = pl.program_id(1)
    n_ch = ((i + 1) * tq + ck - 1) // ck
    w = misc_ref[:, 0:IDX_HEADS]
    row_pos = i * tq + lax.broadcasted_iota(jnp.int32, (tq, ck), 0)
    col_iota = lax.broadcasted_iota(jnp.int32, (tq, ck), 1)

    def score_chunk(c, carry):
        off = pl.multiple_of(c * ck, ck)
        kt = ik_ref[0, pl.ds(off, ck), :]
        acc = jnp.zeros((tq, ck), F32)
        for h in range(IDX_HEADS):
            logit = _dot_nt(iq_ref[0, h], kt)
            acc = acc + w[:, h:h + 1] * jnp.maximum(logit, 0.0)
        score = acc * IDX_SCALE
        bits = pltpu.bitcast(score, jnp.int32)
        key = jnp.where(bits >= 0, bits, -(bits & 0x7FFFFFFF))
        key = jnp.where(col_iota + off <= row_pos, key, KEY_NEG_INF)
        key_ref[:, pl.ds(off, ck)] = key
        return carry

    lax.fori_loop(0, n_ch, score_chunk, 0)

    def count(pred):
        def body(c, acc):
            off = pl.multiple_of(c * ck, ck)
            kc = key_ref[:, pl.ds(off, ck)]
            hit = jnp.where(pred(kc), 1.0, 0.0)
            for j in range(ck // LANES):
                acc = acc + hit[:, j * LANES:(j + 1) * LANES]
            return acc
        acc = lax.fori_loop(0, n_ch, body, jnp.zeros((tq, LANES), F32))
        return jnp.sum(acc, axis=1, keepdims=True)

    def bit_step(it, t_u):
        bit = lax.shift_left(jnp.int32(1), 31 - it)
        cand_u = t_u | bit
        cand = cand_u ^ INT_MIN
        cnt = count(lambda kc: kc >= cand)
        return jnp.where(cnt >= n_keep, cand_u, t_u)

    t_u = lax.fori_loop(0, 32, bit_step, jnp.zeros((tq, 1), jnp.int32))
    thr = t_u ^ INT_MIN
    cnt_gt = count(lambda kc: kc > thr)
    cnt_ge = count(lambda kc: kc >= thr)
    need = n_keep - cnt_gt
    tie = jnp.where((cnt_ge - cnt_gt > need) & (thr > KEY_NEG_INF), 1, 0)
    any_tie = jnp.max(tie) > 0
    thr_eff = jnp.maximum(thr, KEY_NEG_INF + 1)

    @pl.when(jnp.logical_not(any_tie))
    def _():
        def body(c, carry):
            off = pl.multiple_of(c * ck, ck)
            kc = key_ref[:, pl.ds(off, ck)]
            mask_ref[0, :, pl.ds(off, ck)] = jnp.where(kc >= thr_eff, 1, 0).astype(jnp.int8)
            return carry
        lax.fori_loop(0, n_ch, body, 0)

    @pl.when(any_tie)
    def _():
        upper = jnp.where(lax.broadcasted_iota(jnp.int32, (ck, ck), 0)
                          < lax.broadcasted_iota(jnp.int32, (ck, ck), 1), 1.0, 0.0).astype(BF16)
        need_f = need.astype(F32)

        def body(c, seen):
            off = pl.multiple_of(c * ck, ck)
            kc = key_ref[:, pl.ds(off, ck)]
            eq = jnp.where(kc == thr, 1.0, 0.0)
            rank = seen + _dot(eq.astype(BF16), upper)
            take_eq = jnp.where(rank < need_f, eq, 0.0)
            sel = jnp.where(kc > thr, 1.0, take_eq)
            sel = jnp.where(kc > KEY_NEG_INF, sel, 0.0)
            mask_ref[0, :, pl.ds(off, ck)] = sel.astype(jnp.int32).astype(jnp.int8)
            return seen + jnp.sum(eq, axis=1, keepdims=True)
        lax.fori_loop(0, n_ch, body, jnp.zeros((tq, 1), F32))

    def zero_tail(c, carry):
        off = pl.multiple_of(c * ck, ck)
        mask_ref[0, :, pl.ds(off, ck)] = jnp.zeros((tq, ck), jnp.int8)
        return carry
    lax.fori_loop(n_ch, seq // ck, zero_tail, 0)


def _select(iq_hm, ik, a_proj, bsz, seq, n_keep, tq=128, ck=512):
    ck = min(ck, seq)
    tpb = seq // tq
    misc_col = (3 * ATTN_WIDTH + IDX_HEADS * IDX_DIM) // LANES
    return pl.pallas_call(
        functools.partial(_select_kernel, tq=tq, ck=ck, n_keep=n_keep, seq=seq),
        grid=(bsz, tpb),
        in_specs=[pl.BlockSpec((1, IDX_HEADS, tq, IDX_DIM), lambda b, i: (b, 0, i, 0)),
                  pl.BlockSpec((1, seq, IDX_DIM), lambda b, i: (b, 0, 0)),
                  pl.BlockSpec((tq, LANES), lambda b, i: (b * tpb + i, misc_col))],
        out_specs=pl.BlockSpec((1, tq, seq), lambda b, i: (b, i, 0)),
        out_shape=jax.ShapeDtypeStruct((bsz, seq, seq), jnp.int8),
        scratch_shapes=[pltpu.VMEM((tq, seq), jnp.int32)],
        compiler_params=_cparams(2),
    )(iq_hm, ik, a_proj)


def _attn_kernel(q_ref, k_ref, v_ref, mask_ref, o_ref, m_ref, l_ref, acc_ref, *, tq, tk):
    i = pl.program_id(1)
    j = pl.program_id(2)
    last_j = ((i + 1) * tq - 1) // tk

    @pl.when(j == 0)
    def _():
        m_ref[...] = jnp.full(m_ref.shape, NEG_BIG, F32)
        l_ref[...] = jnp.zeros(l_ref.shape, F32)
        acc_ref[...] = jnp.zeros(acc_ref.shape, F32)

    @pl.when(j <= last_j)
    def _():
        bias = jnp.where(mask_ref[0].astype(jnp.int32) != 0, 0.0, NEG_BIG)

        def head(h, carry):
            s = _dot_nt(q_ref[0, h], k_ref[0, h]) + bias
            m_old = m_ref[h]
            m_new = jnp.maximum(m_old, jnp.max(s, axis=1, keepdims=True))
            alpha = jnp.exp(m_old - m_new)
            p = jnp.exp(s - m_new)
            l_ref[h] = alpha * l_ref[h] + jnp.sum(p, axis=1, keepdims=True)
            acc_ref[h] = alpha * acc_ref[h] + _dot(p.astype(BF16), v_ref[0, h])
            m_ref[h] = m_new
            return carry
        lax.fori_loop(0, ATTN_HEADS, head, 0)

    @pl.when(j == last_j)
    def _():
        for hp in range(ATTN_HEADS // 2):
            o0 = acc_ref[2 * hp] / l_ref[2 * hp]
            o1 = acc_ref[2 * hp + 1] / l_ref[2 * hp + 1]
            o_ref[0, :, hp * LANES:(hp + 1) * LANES] = jnp.concatenate([o0, o1], axis=1).astype(o_ref.dtype)


def _attention(q_hm, k_hm, v_hm, mask, bsz, seq, tq=256, tk=512):
    tk = min(tk, seq)
    tq = min(tq, tk)
    nq, nk = seq // tq, seq // tk
    last = lambda i: ((i + 1) * tq - 1) // tk
    kv_spec = pl.BlockSpec((1, ATTN_HEADS, tk, HEAD_DIM), lambda b, i, j: (b, 0, jnp.minimum(j, last(i)), 0))
    return pl.pallas_call(
        functools.partial(_attn_kernel, tq=tq, tk=tk),
        grid=(bsz, nq, nk),
        in_specs=[pl.BlockSpec((1, ATTN_HEADS, tq, HEAD_DIM), lambda b, i, j: (b, 0, i, 0)),
                  kv_spec, kv_spec,
                  pl.BlockSpec((1, tq, tk), lambda b, i, j: (b, i, jnp.minimum(j, last(i))))],
        out_specs=pl.BlockSpec((1, tq, ATTN_WIDTH), lambda b, i, j: (b, i, 0)),
        out_shape=jax.ShapeDtypeStruct((bsz, seq, ATTN_WIDTH), BF16),
        scratch_shapes=[pltpu.VMEM((ATTN_HEADS, tq, 1), F32),
                        pltpu.VMEM((ATTN_HEADS, tq, 1), F32),
                        pltpu.VMEM((ATTN_HEADS, tq, HEAD_DIM), F32)],
        compiler_params=_cparams(3),
    )(q_hm, k_hm, v_hm, mask)


def _ssd_kernel(z_ref, xs_ref, bc_ref, dt_ref, cwx_ref, cbx_ref, cwb_ref, cbb_ref, dtb_ref, alog_ref,
                dskip_ref, gain_ref, expand_ref, o_ref, extx_ref, extb_ref, state_ref, y_ref, *, t):
    c = pl.program_id(1)
    halo = SUBLANES

    @pl.when(c == 0)
    def _():
        extx_ref[0:halo, :] = jnp.zeros((halo, D_INNER), F32)
        extb_ref[0:halo, :] = jnp.zeros((halo, BC_WIDTH), F32)
        state_ref[...] = jnp.zeros(state_ref.shape, F32)

    def conv_silu(ext_ref, raw_ref, w_ref, b_ref):
        ext_ref[halo:halo + t, :] = raw_ref[...]
        y = b_ref[...] + w_ref[SSM_CONV - 1:SSM_CONV, :] * ext_ref[halo:halo + t, :]
        for k in range(SSM_CONV - 1):
            shift = SSM_CONV - 1 - k
            y = y + w_ref[k:k + 1, :] * ext_ref[halo - shift:halo - shift + t, :]
        ext_ref[0:halo, :] = ext_ref[t:t + halo, :]
        return _silu(y)

    xs = conv_silu(extx_ref, xs_ref, cwx_ref, cbx_ref)
    bc = conv_silu(extb_ref, bc_ref, cwb_ref, cbb_ref)

    dtv = dt_ref[...] + dtb_ref[...]
    dt = jnp.maximum(dtv, 0.0) + jnp.log1p(jnp.exp(-jnp.abs(dtv)))
    a = dt * (-jnp.exp(alog_ref[...]))

    r_io = lax.broadcasted_iota(jnp.int32, (t, t), 0)
    c_io = lax.broadcasted_iota(jnp.int32, (t, t), 1)
    tri = r_io >= c_io
    upper_incl = jnp.where(r_io <= c_io, 1.0, 0.0).astype(BF16)
    lower_incl = jnp.where(tri, 1.0, 0.0).astype(BF16)
    a_cs = _dot3_left(lower_incl, a)
    a_cs_t = _dot3(a.T, upper_incl)
    expand = expand_ref[...]
    dt_x = _dot3(dt, expand)
    acs_x = _dot3(a_cs, expand)
    alast_x = acs_x[t - 1:t, :]
    xdt = xs * dt_x
    e_start = jnp.exp(acs_x)
    xw = (xdt * jnp.exp(alast_x - acs_x)).astype(BF16)
    xdt_b = xdt.astype(BF16)
    chunk_decay = jnp.exp(alast_x)
    left_half = lax.broadcasted_iota(jnp.int32, (t, LANES), 1) < SSM_HEAD_DIM

    gw = SSM_GROUPS * D_STATE
    for g in range(SSM_GROUPS):
        bm = bc[:, g * D_STATE:(g + 1) * D_STATE]
        cm = bc[:, gw + g * D_STATE:gw + (g + 1) * D_STATE].astype(BF16)
        bm_t = bm.T.astype(BF16)
        cb = _dot_nt(cm, bm.astype(BF16))
        for p in range(HEADS_PER_GROUP // 2):
            pair = g * (HEADS_PER_GROUP // 2) + p
            lanes = slice(pair * LANES, (pair + 1) * LANES)
            xp = xdt_b[:, lanes]
            ys = []
            for e in range(2):
                h = 2 * pair + e
                seg = a_cs[:, h:h + 1] - a_cs_t[h:h + 1, :]
                dec = jnp.exp(jnp.where(tri, seg, -jnp.inf))
                ys.append(_dot((cb * dec).astype(BF16), xp))
            y_diag = jnp.where(left_half, ys[0], ys[1])
            st = state_ref[pair]
            y_off = _dot(cm, st.astype(BF16)) * e_start[:, lanes]
            y_ref[:, lanes] = y_diag + y_off
            state_ref[pair] = st * chunk_decay[:, lanes] + _dot(bm_t, xw[:, lanes])

    y = (y_ref[...] + xs * dskip_ref[...]) * _silu(z_ref[...])
    gwid = D_INNER // SSM_GROUPS
    for g in range(SSM_GROUPS):
        yg = y[:, g * gwid:(g + 1) * gwid]
        ms = jnp.mean(yg * yg, axis=-1, keepdims=True)
        o_ref[0, :, g * gwid:(g + 1) * gwid] = (yg * lax.rsqrt(ms + EPS)
                                                 * gain_ref[:, g * gwid:(g + 1) * gwid]).astype(o_ref.dtype)


def _ssd(b_proj, cwx, cbx, cwb, cbb, dtb, alog, dskip_x, gain, expand, bsz, seq):
    t = SSM_CHUNK
    nc = seq // t
    row = lambda b, c: b * nc + c
    const = lambda shape: pl.BlockSpec(shape, lambda b, c: (0,) * len(shape))
    return pl.pallas_call(
        functools.partial(_ssd_kernel, t=t),
        grid=(bsz, nc),
        in_specs=[pl.BlockSpec((t, D_INNER), lambda b, c: (row(b, c), 0)),
                  pl.BlockSpec((t, D_INNER), lambda b, c: (row(b, c), 1)),
                  pl.BlockSpec((t, BC_WIDTH), lambda b, c: (row(b, c), 2 * D_INNER // BC_WIDTH)),
                  pl.BlockSpec((t, LANES), lambda b, c: (row(b, c), (2 * D_INNER + BC_WIDTH) // LANES)),
                  const((SSM_CONV, D_INNER)), const((1, D_INNER)),
                  const((SSM_CONV, BC_WIDTH)), const((1, BC_WIDTH)),
                  const((1, LANES)), const((1, LANES)),
                  const((1, D_INNER)), const((1, D_INNER)),
                  const((LANES, D_INNER))],
        out_specs=pl.BlockSpec((1, t, D_INNER), lambda b, c: (b, c, 0)),
        out_shape=jax.ShapeDtypeStruct((bsz, seq, D_INNER), BF16),
        scratch_shapes=[pltpu.VMEM((t + SUBLANES, D_INNER), F32),
                        pltpu.VMEM((t + SUBLANES, BC_WIDTH), F32),
                        pltpu.VMEM((SSM_HEADS // 2, D_STATE, LANES), F32),
                        pltpu.VMEM((t, D_INNER), F32)],
        compiler_params=_cparams(2),
    )(b_proj, b_proj, b_proj, b_proj, cwx, cbx, cwb, cbb, dtb, alog, dskip_x, gain, expand)


def _merge_kernel(x_ref, oa_ref, ys_ref, ga_ref, gm_ref, g1_ref, wa_ref, ws_ref, wo_ref, o_ref):
    o_attn = _dot(oa_ref[...], wa_ref[...])
    o_ssm = _dot(ys_ref[...], ws_ref[...])
    mix = _sigmoid(ga_ref[...]) * o_attn + _sigmoid(gm_ref[...]) * o_ssm
    o_ref[...] = x_ref[...] + g1_ref[0] * _dot(mix.astype(BF16), wo_ref[...])


def _merge(x2, oa2, ys2, c_proj, g1, wa, ws, wo, seq, tm=256):
    m, d = x2.shape
    tpb = seq // tm
    const = lambda shape: pl.BlockSpec(shape, lambda i: (0,) * len(shape))
    return pl.pallas_call(
        _merge_kernel,
        grid=(m // tm,),
        in_specs=[pl.BlockSpec((tm, d), lambda i: (i, 0)),
                  pl.BlockSpec((tm, ATTN_WIDTH), lambda i: (i, 0)),
                  pl.BlockSpec((tm, D_INNER), lambda i: (i, 0)),
                  pl.BlockSpec((tm, d), lambda i: (i, 0)),
                  pl.BlockSpec((tm, d), lambda i: (i, 1)),
                  pl.BlockSpec((1, 1, d), lambda i: (i // tpb, 0, 0)),
                  const((ATTN_WIDTH, d)), const((D_INNER, d)), const((d, d))],
        out_specs=pl.BlockSpec((tm, d), lambda i: (i, 0)),
        out_shape=jax.ShapeDtypeStruct((m, d), F32),
        compiler_params=_cparams(1),
    )(x2, oa2, ys2, c_proj, c_proj, g1, wa, ws, wo)


def _ffn_kernel(x_ref, g_ref, sc_ref, sh_ref, g2_ref, wup_ref, cw_ref, cb_ref, wdn_ref, o_ref, ext_ref, *, tm):
    i = pl.program_id(1)
    halo = SUBLANES

    @pl.when(i == 0)
    def _():
        ext_ref[0:halo, :] = jnp.zeros((halo, 2 * D_FF), F32)

    x = x_ref[...]
    h = _norm_mod(x, g_ref[...], sc_ref[0], sh_ref[0]).astype(BF16)
    for (c0, c1) in _col_chunks(2 * D_FF, 1408):
        ext_ref[halo:halo + tm, c0:c1] = _dot(h, wup_ref[:, c0:c1])

    def conv(c0, c1):
        u = cb_ref[:, c0:c1] + cw_ref[FFN_CONV - 1:FFN_CONV, c0:c1] * ext_ref[halo:halo + tm, c0:c1]
        for k in range(FFN_CONV - 1):
            shift = FFN_CONV - 1 - k
            u = u + cw_ref[k:k + 1, c0:c1] * ext_ref[halo - shift:halo - shift + tm, c0:c1]
        return u

    out = jnp.zeros((tm, D_MODEL), F32)
    for (c0, c1) in _col_chunks(D_FF, 1408):
        act = _silu(conv(D_FF + c0, D_FF + c1)) * conv(c0, c1)
        out = out + _dot(act.astype(BF16), wdn_ref[c0:c1, :])
    o_ref[...] = x + g2_ref[0] * out
    ext_ref[0:halo, :] = ext_ref[tm:tm + halo, :]


def _ffn(x2, g, sc, sh, g2, wup, cw, cb, wdn, bsz, seq, tm=256):
    m, d = x2.shape
    tpb = seq // tm
    row = lambda b, i: b * tpb + i
    const = lambda shape: pl.BlockSpec(shape, lambda b, i: (0,) * len(shape))
    return pl.pallas_call(
        functools.partial(_ffn_kernel, tm=tm),
        grid=(bsz, tpb),
        in_specs=[pl.BlockSpec((tm, d), lambda b, i: (row(b, i), 0)),
                  const((1, d)),
                  pl.BlockSpec((1, 1, d), lambda b, i: (b, 0, 0)),
                  pl.BlockSpec((1, 1, d), lambda b, i: (b, 0, 0)),
                  pl.BlockSpec((1, 1, d), lambda b, i: (b, 0, 0)),
                  const((d, 2 * D_FF)), const((FFN_CONV, 2 * D_FF)), const((1, 2 * D_FF)),
                  const((D_FF, d))],
        out_specs=pl.BlockSpec((tm, d), lambda b, i: (row(b, i), 0)),
        out_shape=jax.ShapeDtypeStruct((m, d), F32),
        scratch_shapes=[pltpu.VMEM((tm + SUBLANES, 2 * D_FF), F32)],
        compiler_params=_cparams(2),
    )(x2, g, sc, sh, g2, wup, cw, cb, wdn)


def kernel(x, c, positions, w_ada, b_ada, norm1_g, w_in, q_norm_g, k_norm_g, ssm_conv_w, ssm_conv_b, dt_bias,
           a_log, d_skip, ssm_norm_g, w_attn_o, w_ssm_o, w_out, norm2_g, w_up, ffn_conv_w, ffn_conv_b, w_down):
    bsz, seq, d = x.shape
    depth = w_ada.shape[0]
    n_keep = min(TOPK_MAX, seq // 4)
    m = bsz * seq

    inv = 1.0 / (ROPE_THETA ** (jnp.arange(0, HEAD_DIM, 2, dtype=F32) / HEAD_DIM))
    ang = positions.astype(F32)[..., None] * inv
    cos64 = jnp.concatenate([jnp.cos(ang), jnp.cos(ang)], axis=-1).reshape(m, HEAD_DIM)
    sin64 = jnp.concatenate([-jnp.sin(ang), jnp.sin(ang)], axis=-1).reshape(m, HEAD_DIM)

    sp = np.cumsum((0, ATTN_WIDTH, ATTN_WIDTH, ATTN_WIDTH, IDX_HEADS * IDX_DIM, IDX_DIM, IDX_HEADS, D_INNER,
                    D_INNER + BC_WIDTH, SSM_HEADS, D_MODEL, D_MODEL))
    col = lambda k: w_in[:, :, sp[k]:sp[k + 1]]
    zpad = lambda n: jnp.zeros((depth, d, n), w_in.dtype)
    w_a = jnp.concatenate([col(0), col(1), col(2), col(3), col(5), zpad(LANES - IDX_DIM - IDX_HEADS), col(4)],
                          axis=-1).astype(BF16)
    w_b = jnp.concatenate([col(6), col(7), col(8), zpad(LANES - SSM_HEADS)], axis=-1).astype(BF16)
    w_c = jnp.concatenate([col(9), col(10)], axis=-1).astype(BF16)
    wa_o, ws_o, wo = w_attn_o.astype(BF16), w_ssm_o.astype(BF16), w_out.astype(BF16)
    wup, wdn = w_up.astype(BF16), w_down.astype(BF16)

    lane_pad = lambda v: jnp.pad(v, ((0, 0), (0, LANES - v.shape[-1])))
    dtb = lane_pad(dt_bias.astype(F32))
    alog = lane_pad(a_log.astype(F32))
    dskip_x = jnp.repeat(d_skip.astype(F32), SSM_HEAD_DIM, axis=-1)
    expand = (jnp.arange(LANES)[:, None] == (jnp.arange(D_INNER)[None, :] // SSM_HEAD_DIM)).astype(BF16)

    c_pad = jnp.pad(c.astype(F32), ((0, SUBLANES - bsz), (0, 0)))
    mod = _adaln(c_pad, w_ada, b_ada)[:, :bsz]

    x2 = x.reshape(m, d)
    for l in range(depth):
        sh1, sc1, g1, sh2, sc2, g2 = [mod[l, :, k * d:(k + 1) * d].reshape(bsz, 1, d) for k in range(6)]
        g_n1 = norm1_g[l].reshape(1, d)
        a_proj = _nm_matmul(x2, g_n1, sc1, sh1, w_a[l], seq)
        b_proj = _nm_matmul(x2, g_n1, sc1, sh1, w_b[l], seq)
        c_proj = _nm_matmul(x2, g_n1, sc1, sh1, w_c[l], seq)

        q_hm, k_hm, v_hm, iq_hm, ik = _prep(a_proj, cos64, sin64, q_norm_g[l].reshape(1, HEAD_DIM),
                                            k_norm_g[l].reshape(1, HEAD_DIM), bsz, seq)
        mask = _select(iq_hm, ik, a_proj, bsz, seq, n_keep)
        o_attn = _attention(q_hm, k_hm, v_hm, mask, bsz, seq)

        y_ssm = _ssd(b_proj, ssm_conv_w[l, :, :D_INNER], ssm_conv_b[l, :D_INNER].reshape(1, D_INNER),
                     ssm_conv_w[l, :, D_INNER:], ssm_conv_b[l, D_INNER:].reshape(1, BC_WIDTH),
                     dtb[l].reshape(1, LANES), alog[l].reshape(1, LANES), dskip_x[l].reshape(1, D_INNER),
                     ssm_norm_g[l].reshape(1, D_INNER), expand, bsz, seq)

        x2 = _merge(x2, o_attn.reshape(m, ATTN_WIDTH), y_ssm.reshape(m, D_INNER), c_proj, g1,
                    wa_o[l], ws_o[l], wo[l], seq)
        x2 = _ffn(x2, norm2_g[l].reshape(1, d), sc2, sh2, g2, wup[l], ffn_conv_w[l],
                  ffn_conv_b[l].reshape(1, 2 * D_FF), wdn[l], bsz, seq)
    return x2.reshape(bsz, seq, d)
```

```python
import functools

import numpy as np
import jax
import jax.numpy as jnp
from jax import lax
from jax.experimental import pallas as pl
from jax.experimental.pallas import tpu as pltpu

F32 = jnp.float32
BF16 = jnp.bfloat16

D_MODEL = 1024
ATTN_HEADS = 16
HEAD_DIM = 64
ATTN_WIDTH = ATTN_HEADS * HEAD_DIM
IDX_HEADS = 8
IDX_DIM = 64
IDX_SCALE = (IDX_HEADS ** -0.5) * (IDX_DIM ** -0.5)
TOPK_MAX = 256
ROPE_THETA = 10000.0
D_INNER = 2 * D_MODEL
SSM_HEAD_DIM = 64
SSM_HEADS = D_INNER // SSM_HEAD_DIM
SSM_GROUPS = 4
HEADS_PER_GROUP = SSM_HEADS // SSM_GROUPS
D_STATE = 128
SSM_CONV = 4
SSM_CHUNK = 256
BC_WIDTH = 2 * SSM_GROUPS * D_STATE
D_FF = 2816
FFN_CONV = 3
EPS = 1e-6

LANES = 128
SUBLANES = 8
VMEM_LIMIT = 56 * 1024 * 1024

A_WIDTH = 3 * ATTN_WIDTH + IDX_HEADS * IDX_DIM + LANES
B_WIDTH = 2 * D_INNER + BC_WIDTH + LANES
C_WIDTH = 2 * D_MODEL

NEG_BIG = -1e30
INT_MIN = -2 ** 31
KEY_NEG_INF = -0x7F800000


def _cparams(n_axes):
    return pltpu.CompilerParams(dimension_semantics=("arbitrary",) * n_axes,
                                vmem_limit_bytes=VMEM_LIMIT)


def _sigmoid(x):
    return 1.0 / (1.0 + jnp.exp(-x))


def _silu(x):
    return x * _sigmoid(x)


def _dot(a, b):
    return jnp.dot(a, b, preferred_element_type=F32)


def _dot_nt(a, b):
    return lax.dot_general(a, b, (((1,), (1,)), ((), ())), preferred_element_type=F32)


def _split3(a):
    hi = a.astype(BF16)
    r1 = a - hi.astype(F32)
    mid = r1.astype(BF16)
    lo = (r1 - mid.astype(F32)).astype(BF16)
    return hi, mid, lo


def _dot3(a, b01):
    hi, mid, lo = _split3(a)
    return _dot(hi, b01) + _dot(mid, b01) + _dot(lo, b01)


def _dot3_left(b01, a):
    hi, mid, lo = _split3(a)
    return _dot(b01, hi) + _dot(b01, mid) + _dot(b01, lo)


def _adaln_kernel(c_ref, w_ref, b_ref, o_ref):
    c = c_ref[...]
    act = _silu(c).astype(BF16)
    o_ref[0] = _dot(act, w_ref[0].astype(BF16)) + b_ref[0]


def _adaln(c_pad, w_ada, b_ada):
    depth, d, n = w_ada.shape
    tn = 1536
    return pl.pallas_call(
        _adaln_kernel,
        grid=(depth, n // tn),
        in_specs=[pl.BlockSpec((SUBLANES, d), lambda l, j: (0, 0)),
                  pl.BlockSpec((1, d, tn), lambda l, j: (l, 0, j)),
                  pl.BlockSpec((1, 1, tn), lambda l, j: (l, 0, j))],
        out_specs=pl.BlockSpec((1, SUBLANES, tn), lambda l, j: (l, 0, j)),
        out_shape=jax.ShapeDtypeStruct((depth, SUBLANES, n), F32),
        compiler_params=_cparams(2),
    )(c_pad, w_ada, b_ada.reshape(depth, 1, n))


def _norm_mod(x, g, sc, sh):
    ms = jnp.mean(x * x, axis=-1, keepdims=True)
    h = x * lax.rsqrt(ms + EPS) * g
    return h * (1.0 + sc) + sh


def _nm_matmul_kernel(x_ref, g_ref, sc_ref, sh_ref, w_ref, o_ref, *, chunks):
    h = _norm_mod(x_ref[...], g_ref[...], sc_ref[0], sh_ref[0]).astype(BF16)
    for (c0, c1) in chunks:
        o_ref[:, c0:c1] = _dot(h, w_ref[:, c0:c1]).astype(o_ref.dtype)


def _col_chunks(n, step=1024):
    out, c = [], 0
    while c < n:
        out.append((c, min(c + step, n)))
        c += step
    return tuple(out)


def _nm_matmul(x2, g, sc, sh, w, seq, tm=256):
    m, d = x2.shape
    n = w.shape[1]
    tpb = seq // tm
    return pl.pallas_call(
        functools.partial(_nm_matmul_kernel, chunks=_col_chunks(n)),
        grid=(m // tm,),
        in_specs=[pl.BlockSpec((tm, d), lambda i: (i, 0)),
                  pl.BlockSpec((1, d), lambda i: (0, 0)),
                  pl.BlockSpec((1, 1, d), lambda i: (i // tpb, 0, 0)),
                  pl.BlockSpec((1, 1, d), lambda i: (i // tpb, 0, 0)),
                  pl.BlockSpec((d, n), lambda i: (0, 0))],
        out_specs=pl.BlockSpec((tm, n), lambda i: (i, 0)),
        out_shape=jax.ShapeDtypeStruct((m, n), F32),
        compiler_params=_cparams(1),
    )(x2, g, sc, sh, w)


def _rope64(xh, cos, sins):
    rolled = jnp.concatenate([xh[:, HEAD_DIM // 2:], xh[:, :HEAD_DIM // 2]], axis=1)
    return xh * cos + rolled * sins


def _prep_kernel(q_ref, k_ref, v_ref, iq_ref, misc_ref, cos_ref, sin_ref, qg_ref, kg_ref,
                 qo_ref, ko_ref, vo_ref, iqo_ref, iko_ref, iwo_ref):
    cos = cos_ref[...]
    sins = sin_ref[...]

    def normed(xh, g):
        ms = jnp.mean(xh * xh, axis=-1, keepdims=True)
        return xh * lax.rsqrt(ms + EPS) * g

    def pair_t(x0, x1):
        return jnp.concatenate([x0, x1], axis=1).T.astype(BF16)

    qg = qg_ref[...]
    kg = kg_ref[...]
    for hp in range(ATTN_HEADS // 2):
        qs = []
        for h in (2 * hp, 2 * hp + 1):
            sl = slice(h * HEAD_DIM, (h + 1) * HEAD_DIM)
            qs.append(_rope64(normed(q_ref[:, sl], qg), cos, sins) * (HEAD_DIM ** -0.5))
            ko_ref[0, h] = _rope64(normed(k_ref[:, sl], kg), cos, sins).astype(BF16)
        qo_ref[0, hp * LANES:(hp + 1) * LANES, :] = pair_t(qs[0], qs[1])
    vo_ref[0] = v_ref[...].T.astype(BF16)
    for hp in range(IDX_HEADS // 2):
        iqs = [_rope64(iq_ref[:, h * IDX_DIM:(h + 1) * IDX_DIM], cos, sins) for h in (2 * hp, 2 * hp + 1)]
        iqo_ref[0, hp * LANES:(hp + 1) * LANES, :] = pair_t(iqs[0], iqs[1])
    misc = misc_ref[...]
    iko_ref[0] = _rope64(misc[:, IDX_DIM:2 * IDX_DIM], cos, sins).astype(BF16)
    iwo_ref[0] = misc.T[0:IDX_HEADS, :]


def _prep(a_proj, cos64, sin64, qg, kg, bsz, seq, tm=256):
    tpb = seq // tm
    row = lambda b, i: b * tpb + i
    aw = ATTN_WIDTH
    iqw = IDX_HEADS * IDX_DIM
    return pl.pallas_call(
        _prep_kernel,
        grid=(bsz, tpb),
        in_specs=[pl.BlockSpec((tm, aw), lambda b, i: (row(b, i), 0)),
                  pl.BlockSpec((tm, aw), lambda b, i: (row(b, i), 1)),
                  pl.BlockSpec((tm, aw), lambda b, i: (row(b, i), 2)),
                  pl.BlockSpec((tm, iqw), lambda b, i: (row(b, i), 3 * aw // iqw)),
                  pl.BlockSpec((tm, LANES), lambda b, i: (row(b, i), (3 * aw + iqw) // LANES)),
                  pl.BlockSpec((tm, HEAD_DIM), lambda b, i: (row(b, i), 0)),
                  pl.BlockSpec((tm, HEAD_DIM), lambda b, i: (row(b, i), 0)),
                  pl.BlockSpec((1, HEAD_DIM), lambda b, i: (0, 0)),
                  pl.BlockSpec((1, HEAD_DIM), lambda b, i: (0, 0))],
        out_specs=[pl.BlockSpec((1, aw, tm), lambda b, i: (b, 0, i)),
                   pl.BlockSpec((1, ATTN_HEADS, tm, HEAD_DIM), lambda b, i: (b, 0, i, 0)),
                   pl.BlockSpec((1, aw, tm), lambda b, i: (b, 0, i)),
                   pl.BlockSpec((1, iqw, tm), lambda b, i: (b, 0, i)),
                   pl.BlockSpec((1, tm, IDX_DIM), lambda b, i: (b, i, 0)),
                   pl.BlockSpec((1, IDX_HEADS, tm), lambda b, i: (b, 0, i))],
        out_shape=[jax.ShapeDtypeStruct((bsz, aw, seq), BF16),
                   jax.ShapeDtypeStruct((bsz, ATTN_HEADS, seq, HEAD_DIM), BF16),
                   jax.ShapeDtypeStruct((bsz, aw, seq), BF16),
                   jax.ShapeDtypeStruct((bsz, iqw, seq), BF16),
                   jax.ShapeDtypeStruct((bsz, seq, IDX_DIM), BF16),
                   jax.ShapeDtypeStruct((bsz, IDX_HEADS, seq), F32)],
        compiler_params=_cparams(2),
    )(a_proj, a_proj, a_proj, a_proj, a_proj, cos64, sin64, qg, kg)


def _select_kernel(iq_ref, ik_ref, iw_ref, mask_ref, key_ref, *, tq, ck, n_keep, seq):
    i = pl.program_id(1)
    n_ch = ((i + 1) * tq + ck - 1) // ck
    w = iw_ref[0]
    q_pos = i * tq + lax.broadcasted_iota(jnp.int32, (ck, tq), 1)
    k_iota = lax.broadcasted_iota(jnp.int32, (ck, tq), 0)

    def score_chunk(c, carry):
        off = pl.multiple_of(c * ck, ck)
        kt = ik_ref[0, pl.ds(off, ck), :]
        acc = jnp.zeros((ck, tq), F32)
        for h in range(IDX_HEADS):
            logit = _dot(kt, iq_ref[0, h * IDX_DIM:(h + 1) * IDX_DIM, :])
            acc = acc + w[h:h + 1, :] * jnp.maximum(logit, 0.0)
        score = acc * IDX_SCALE
        bits = pltpu.bitcast(score, jnp.int32)
        key = jnp.where(bits >= 0, bits, -(bits & 0x7FFFFFFF))
        key = jnp.where(k_iota + off <= q_pos, key, KEY_NEG_INF)
        key_ref[pl.ds(off, ck), :] = key
        return carry

    lax.fori_loop(0, n_ch, score_chunk, 0)

    cnt_rows = 4 * SUBLANES

    def count(pred):
        def body(c, acc):
            off = pl.multiple_of(c * ck, ck)
            hit = jnp.where(pred(key_ref[pl.ds(off, ck), :]), 1.0, 0.0)
            return acc + jnp.sum(hit.reshape(ck // cnt_rows, cnt_rows, tq), axis=0)
        acc = lax.fori_loop(0, n_ch, body, jnp.zeros((cnt_rows, tq), F32))
        return jnp.sum(acc, axis=0, keepdims=True)

    def bit_step(it, t_u):
        bit = lax.shift_left(jnp.int32(1), 31 - it)
        cand_u = t_u | bit
        cand = cand_u ^ INT_MIN
        cnt = count(lambda kc: kc >= cand)
        return jnp.where(cnt >= n_keep, cand_u, t_u)

    t_u = lax.fori_loop(0, 32, bit_step, jnp.zeros((1, tq), jnp.int32))
    thr = t_u ^ INT_MIN
    cnt_gt = count(lambda kc: kc > thr)
    cnt_ge = count(lambda kc: kc >= thr)
    need = n_keep - cnt_gt
    tie = jnp.where((cnt_ge - cnt_gt > need) & (thr > KEY_NEG_INF), 1, 0)
    any_tie = jnp.max(tie) > 0
    thr_eff = jnp.maximum(thr, KEY_NEG_INF + 1)

    @pl.when(jnp.logical_not(any_tie))
    def _():
        def body(c, carry):
            off = pl.multiple_of(c * ck, ck)
            kc = key_ref[pl.ds(off, ck), :]
            mask_ref[0, pl.ds(off, ck), :] = jnp.where(kc >= thr_eff, 1, 0).astype(jnp.int8)
            return carry
        lax.fori_loop(0, n_ch, body, 0)

    @pl.when(any_tie)
    def _():
        lower = jnp.where(lax.broadcasted_iota(jnp.int32, (ck, ck), 1)
                          < lax.broadcasted_iota(jnp.int32, (ck, ck), 0), 1.0, 0.0).astype(BF16)

        def body(c, seen):
            off = pl.multiple_of(c * ck, ck)
            kc = key_ref[pl.ds(off, ck), :]
            eq = jnp.where(kc == thr, 1.0, 0.0)
            rank = seen + _dot(lower, eq.astype(BF16))
            take_eq = jnp.where(rank < need, eq, 0.0)
            sel = jnp.where(kc > thr, 1.0, take_eq)
            sel = jnp.where(kc > KEY_NEG_INF, sel, 0.0)
            mask_ref[0, pl.ds(off, ck), :] = sel.astype(jnp.int32).astype(jnp.int8)
            return seen + jnp.sum(eq, axis=0, keepdims=True)
        lax.fori_loop(0, n_ch, body, jnp.zeros((1, tq), F32))

    def zero_tail(c, carry):
        off = pl.multiple_of(c * ck, ck)
        mask_ref[0, pl.ds(off, ck), :] = jnp.zeros((ck, tq), jnp.int8)
        return carry
    lax.fori_loop(n_ch, seq // ck, zero_tail, 0)


def _select(iq_t, ik, iw_t, bsz, seq, n_keep, tq=256, ck=512):
    ck = min(ck, seq)
    return pl.pallas_call(
        functools.partial(_select_kernel, tq=tq, ck=ck, n_keep=n_keep, seq=seq),
        grid=(bsz, seq // tq),
        in_specs=[pl.BlockSpec((1, IDX_HEADS * IDX_DIM, tq), lambda b, i: (b, 0, i)),
                  pl.BlockSpec((1, seq, IDX_DIM), lambda b, i: (b, 0, 0)),
                  pl.BlockSpec((1, IDX_HEADS, tq), lambda b, i: (b, 0, i))],
        out_specs=pl.BlockSpec((1, seq, tq), lambda b, i: (b, 0, i)),
        out_shape=jax.ShapeDtypeStruct((bsz, seq, seq), jnp.int8),
        scratch_shapes=[pltpu.VMEM((seq, tq), jnp.int32)],
        compiler_params=_cparams(2),
    )(iq_t, ik, iw_t)


def _attn_kernel(q_ref, k_ref, v_ref, mask_ref, o_ref, m_ref, l_ref, acc_ref, bias_ref, s0_ref, s1_ref, *, tq, tk):
    i = pl.program_id(1)
    j = pl.program_id(2)
    last_j = ((i + 1) * tq - 1) // tk

    @pl.when(j == 0)
    def _():
        m_ref[...] = jnp.full(m_ref.shape, NEG_BIG, F32)
        l_ref[...] = jnp.zeros(l_ref.shape, F32)
        acc_ref[...] = jnp.zeros(acc_ref.shape, F32)

    @pl.when(j <= last_j)
    def _():
        bias_ref[...] = jnp.where(mask_ref[0].astype(jnp.int32) != 0, 0.0, NEG_BIG)

        def scores(h, s_ref):
            rows = pl.ds(pl.multiple_of(h * HEAD_DIM, HEAD_DIM), HEAD_DIM)
            s_ref[...] = _dot(k_ref[0, h], q_ref[0, rows, :]) + bias_ref[...]

        def softmax_pv(h, s_ref):
            rows = pl.ds(pl.multiple_of(h * HEAD_DIM, HEAD_DIM), HEAD_DIM)
            s = s_ref[...]
            m_old = m_ref[pl.ds(h, 1), :]
            m_new = jnp.maximum(m_old, jnp.max(s, axis=0, keepdims=True))
            alpha = jnp.exp(m_old - m_new)
            p = jnp.exp(s - m_new)
            l_ref[pl.ds(h, 1), :] = alpha * l_ref[pl.ds(h, 1), :] + jnp.sum(p, axis=0, keepdims=True)
            acc_ref[rows, :] = alpha * acc_ref[rows, :] + _dot(v_ref[0, rows, :], p.astype(BF16))
            m_ref[pl.ds(h, 1), :] = m_new

        scores(0, s0_ref)

        def head_pair(g, carry):
            h = 2 * g
            scores(h + 1, s1_ref)
            softmax_pv(h, s0_ref)
            scores(jnp.minimum(h + 2, ATTN_HEADS - 1), s0_ref)
            softmax_pv(h + 1, s1_ref)
            return carry
        lax.fori_loop(0, ATTN_HEADS // 2, head_pair, 0)

    @pl.when(j == last_j)
    def _():
        for h in range(ATTN_HEADS):
            rows = slice(h * HEAD_DIM, (h + 1) * HEAD_DIM)
            acc_ref[rows, :] = acc_ref[rows, :] / l_ref[h:h + 1, :]
        o_ref[0] = acc_ref[...].T.astype(o_ref.dtype)


def _attention(q_t, k_hm, v_t, mask_t, bsz, seq, tq=256, tk=512):
    tk = min(tk, seq)
    tq = min(tq, tk)
    nq, nk = seq // tq, seq // tk
    last = lambda i: ((i + 1) * tq - 1) // tk
    kj = lambda i, j: jnp.minimum(j, last(i))
    return pl.pallas_call(
        functools.partial(_attn_kernel, tq=tq, tk=tk),
        grid=(bsz, nq, nk),
        in_specs=[pl.BlockSpec((1, ATTN_WIDTH, tq), lambda b, i, j: (b, 0, i)),
                  pl.BlockSpec((1, ATTN_HEADS, tk, HEAD_DIM), lambda b, i, j: (b, 0, kj(i, j), 0)),
                  pl.BlockSpec((1, ATTN_WIDTH, tk), lambda b, i, j: (b, 0, kj(i, j))),
                  pl.BlockSpec((1, tk, tq), lambda b, i, j: (b, kj(i, j), i))],
        out_specs=pl.BlockSpec((1, tq, ATTN_WIDTH), lambda b, i, j: (b, i, 0)),
        out_shape=jax.ShapeDtypeStruct((bsz, seq, ATTN_WIDTH), BF16),
        scratch_shapes=[pltpu.VMEM((ATTN_HEADS, tq), F32),
                        pltpu.VMEM((ATTN_HEADS, tq), F32),
                        pltpu.VMEM((ATTN_WIDTH, tq), F32),
                        pltpu.VMEM((tk, tq), F32),
                        pltpu.VMEM((tk, tq), F32),
                        pltpu.VMEM((tk, tq), F32)],
        compiler_params=_cparams(3),
    )(q_t, k_hm, v_t, mask_t)


def _ssd_kernel(z_ref, xs_ref, bc_ref, dt_ref, cwx_ref, cbx_ref, cwb_ref, cbb_ref, dtb_ref, alog_ref,
                dskip_ref, gain_ref, expand_ref, o_ref, extx_ref, extb_ref, state_ref, y_ref, *, t):
    c = pl.program_id(1)
    halo = SUBLANES

    @pl.when(c == 0)
    def _():
        extx_ref[0:halo, :] = jnp.zeros((halo, D_INNER), F32)
        extb_ref[0:halo, :] = jnp.zeros((halo, BC_WIDTH), F32)
        state_ref[...] = jnp.zeros(state_ref.shape, F32)

    def conv_silu(ext_ref, raw_ref, w_ref, b_ref):
        ext_ref[halo:halo + t, :] = raw_ref[...]
        y = b_ref[...] + w_ref[SSM_CONV - 1:SSM_CONV, :] * ext_ref[halo:halo + t, :]
        for k in range(SSM_CONV - 1):
            shift = SSM_CONV - 1 - k
            y = y + w_ref[k:k + 1, :] * ext_ref[halo - shift:halo - shift + t, :]
        ext_ref[0:halo, :] = ext_ref[t:t + halo, :]
        return _silu(y)

    xs = conv_silu(extx_ref, xs_ref, cwx_ref, cbx_ref)
    bc = conv_silu(extb_ref, bc_ref, cwb_ref, cbb_ref)

    dtv = dt_ref[...] + dtb_ref[...]
    dt = jnp.maximum(dtv, 0.0) + jnp.log1p(jnp.exp(-jnp.abs(dtv)))
    a = dt * (-jnp.exp(alog_ref[...]))

    r_io = lax.broadcasted_iota(jnp.int32, (t, t), 0)
    c_io = lax.broadcasted_iota(jnp.int32, (t, t), 1)
    tri = r_io >= c_io
    upper_incl = jnp.where(r_io <= c_io, 1.0, 0.0).astype(BF16)
    lower_incl = jnp.where(tri, 1.0, 0.0).astype(BF16)
    a_cs = _dot3_left(lower_incl, a)
    a_cs_t = _dot3(a.T, upper_incl)
    expand = expand_ref[...]
    dt_x = _dot3(dt, expand)
    acs_x = _dot3(a_cs, expand)
    alast_x = acs_x[t - 1:t, :]
    xdt = xs * dt_x
    e_start = jnp.exp(acs_x)
    xw = (xdt * jnp.exp(alast_x - acs_x)).astype(BF16)
    xdt_b = xdt.astype(BF16)
    chunk_decay = jnp.exp(alast_x)
    left_half = lax.broadcasted_iota(jnp.int32, (t, LANES), 1) < SSM_HEAD_DIM

    gw = SSM_GROUPS * D_STATE
    for g in range(SSM_GROUPS):
        bm = bc[:, g * D_STATE:(g + 1) * D_STATE]
        cm = bc[:, gw + g * D_STATE:gw + (g + 1) * D_STATE].astype(BF16)
        bm_t = bm.T.astype(BF16)
        cb = _dot_nt(cm, bm.astype(BF16))
        for p in range(HEADS_PER_GROUP // 2):
            pair = g * (HEADS_PER_GROUP // 2) + p
            lanes = slice(pair * LANES, (pair + 1) * LANES)
            xp = xdt_b[:, lanes]
            ys = []
            for e in range(2):
                h = 2 * pair + e
                seg = a_cs[:, h:h + 1] - a_cs_t[h:h + 1, :]
                dec = jnp.exp(jnp.where(tri, seg, -jnp.inf))
                ys.append(_dot((cb * dec).astype(BF16), xp))
            y_diag = jnp.where(left_half, ys[0], ys[1])
            st = state_ref[pair]
            y_off = _dot(cm, st.astype(BF16)) * e_start[:, lanes]
            y_ref[:, lanes] = y_diag + y_off
            state_ref[pair] = st * chunk_decay[:, lanes] + _dot(bm_t, xw[:, lanes])

    y = (y_ref[...] + xs * dskip_ref[...]) * _silu(z_ref[...])
    gwid = D_INNER // SSM_GROUPS
    for g in range(SSM_GROUPS):
        yg = y[:, g * gwid:(g + 1) * gwid]
        ms = jnp.mean(yg * yg, axis=-1, keepdims=True)
        o_ref[0, :, g * gwid:(g + 1) * gwid] = (yg * lax.rsqrt(ms + EPS)
                                                 * gain_ref[:, g * gwid:(g + 1) * gwid]).astype(o_ref.dtype)


def _ssd(b_proj, cwx, cbx, cwb, cbb, dtb, alog, dskip_x, gain, expand, bsz, seq):
    t = SSM_CHUNK
    nc = seq // t
    row = lambda b, c: b * nc + c
    const = lambda shape: pl.BlockSpec(shape, lambda b, c: (0,) * len(shape))
    return pl.pallas_call(
        functools.partial(_ssd_kernel, t=t),
        grid=(bsz, nc),
        in_specs=[pl.BlockSpec((t, D_INNER), lambda b, c: (row(b, c), 0)),
                  pl.BlockSpec((t, D_INNER), lambda b, c: (row(b, c), 1)),
                  pl.BlockSpec((t, BC_WIDTH), lambda b, c: (row(b, c), 2 * D_INNER // BC_WIDTH)),
                  pl.BlockSpec((t, LANES), lambda b, c: (row(b, c), (2 * D_INNER + BC_WIDTH) // LANES)),
                  const((SSM_CONV, D_INNER)), const((1, D_INNER)),
                  const((SSM_CONV, BC_WIDTH)), const((1, BC_WIDTH)),
                  const((1, LANES)), const((1, LANES)),
                  const((1, D_INNER)), const((1, D_INNER)),
                  const((LANES, D_INNER))],
        out_specs=pl.BlockSpec((1, t, D_INNER), lambda b, c: (b, c, 0)),
        out_shape=jax.ShapeDtypeStruct((bsz, seq, D_INNER), BF16),
        scratch_shapes=[pltpu.VMEM((t + SUBLANES, D_INNER), F32),
                        pltpu.VMEM((t + SUBLANES, BC_WIDTH), F32),
                        pltpu.VMEM((SSM_HEADS // 2, D_STATE, LANES), F32),
                        pltpu.VMEM((t, D_INNER), F32)],
        compiler_params=_cparams(2),
    )(b_proj, b_proj, b_proj, b_proj, cwx, cbx, cwb, cbb, dtb, alog, dskip_x, gain, expand)


def _merge_kernel(x_ref, oa_ref, ys_ref, ga_ref, gm_ref, g1_ref, wa_ref, ws_ref, wo_ref, o_ref):
    o_attn = _dot(oa_ref[...], wa_ref[...])
    o_ssm = _dot(ys_ref[...], ws_ref[...])
    mix = _sigmoid(ga_ref[...]) * o_attn + _sigmoid(gm_ref[...]) * o_ssm
    o_ref[...] = x_ref[...] + g1_ref[0] * _dot(mix.astype(BF16), wo_ref[...])


def _merge(x2, oa2, ys2, c_proj, g1, wa, ws, wo, seq, tm=256):
    m, d = x2.shape
    tpb = seq // tm
    const = lambda shape: pl.BlockSpec(shape, lambda i: (0,) * len(shape))
    return pl.pallas_call(
        _merge_kernel,
        grid=(m // tm,),
        in_specs=[pl.BlockSpec((tm, d), lambda i: (i, 0)),
                  pl.BlockSpec((tm, ATTN_WIDTH), lambda i: (i, 0)),
                  pl.BlockSpec((tm, D_INNER), lambda i: (i, 0)),
                  pl.BlockSpec((tm, d), lambda i: (i, 0)),
                  pl.BlockSpec((tm, d), lambda i: (i, 1)),
                  pl.BlockSpec((1, 1, d), lambda i: (i // tpb, 0, 0)),
                  const((ATTN_WIDTH, d)), const((D_INNER, d)), const((d, d))],
        out_specs=pl.BlockSpec((tm, d), lambda i: (i, 0)),
        out_shape=jax.ShapeDtypeStruct((m, d), F32),
        compiler_params=_cparams(1),
    )(x2, oa2, ys2, c_proj, c_proj, g1, wa, ws, wo)


def _ffn_kernel(x_ref, g_ref, sc_ref, sh_ref, g2_ref, wup_ref, cw_ref, cb_ref, wdn_ref, o_ref, ext_ref, *, tm):
    i = pl.program_id(1)
    halo = SUBLANES

    @pl.when(i == 0)
    def _():
        ext_ref[0:halo, :] = jnp.zeros((halo, 2 * D_FF), F32)

    x = x_ref[...]
    h = _norm_mod(x, g_ref[...], sc_ref[0], sh_ref[0]).astype(BF16)
    for (c0, c1) in _col_chunks(2 * D_FF, 1408):
        ext_ref[halo:halo + tm, c0:c1] = _dot(h, wup_ref[:, c0:c1])

    def conv(c0, c1):
        u = cb_ref[:, c0:c1] + cw_ref[FFN_CONV - 1:FFN_CONV, c0:c1] * ext_ref[halo:halo + tm, c0:c1]
        for k in range(FFN_CONV - 1):
            shift = FFN_CONV - 1 - k
            u = u + cw_ref[k:k + 1, c0:c1] * ext_ref[halo - shift:halo - shift + tm, c0:c1]
        return u

    out = jnp.zeros((tm, D_MODEL), F32)
    for (c0, c1) in _col_chunks(D_FF, 1408):
        act = _silu(conv(D_FF + c0, D_FF + c1)) * conv(c0, c1)
        out = out + _dot(act.astype(BF16), wdn_ref[c0:c1, :])
    o_ref[...] = x + g2_ref[0] * out
    ext_ref[0:halo, :] = ext_ref[tm:tm + halo, :]


def _ffn(x2, g, sc, sh, g2, wup, cw, cb, wdn, bsz, seq, tm=256):
    m, d = x2.shape
    tpb = seq // tm
    row = lambda b, i: b * tpb + i
    const = lambda shape: pl.BlockSpec(shape, lambda b, i: (0,) * len(shape))
    return pl.pallas_call(
        functools.partial(_ffn_kernel, tm=tm),
        grid=(bsz, tpb),
        in_specs=[pl.BlockSpec((tm, d), lambda b, i: (row(b, i), 0)),
                  const((1, d)),
                  pl.BlockSpec((1, 1, d), lambda b, i: (b, 0, 0)),
                  pl.BlockSpec((1, 1, d), lambda b, i: (b, 0, 0)),
                  pl.BlockSpec((1, 1, d), lambda b, i: (b, 0, 0)),
                  const((d, 2 * D_FF)), const((FFN_CONV, 2 * D_FF)), const((1, 2 * D_FF)),
                  const((D_FF, d))],
        out_specs=pl.BlockSpec((tm, d), lambda b, i: (row(b, i), 0)),
        out_shape=jax.ShapeDtypeStruct((m, d), F32),
        scratch_shapes=[pltpu.VMEM((tm + SUBLANES, 2 * D_FF), F32)],
        compiler_params=_cparams(2),
    )(x2, g, sc, sh, g2, wup, cw, cb, wdn)


def kernel(x, c, positions, w_ada, b_ada, norm1_g, w_in, q_norm_g, k_norm_g, ssm_conv_w, ssm_conv_b, dt_bias,
           a_log, d_skip, ssm_norm_g, w_attn_o, w_ssm_o, w_out, norm2_g, w_up, ffn_conv_w, ffn_conv_b, w_down):
    bsz, seq, d = x.shape
    depth = w_ada.shape[0]
    n_keep = min(TOPK_MAX, seq // 4)
    m = bsz * seq

    inv = 1.0 / (ROPE_THETA ** (jnp.arange(0, HEAD_DIM, 2, dtype=F32) / HEAD_DIM))
    ang = positions.astype(F32)[..., None] * inv
    cos64 = jnp.concatenate([jnp.cos(ang), jnp.cos(ang)], axis=-1).reshape(m, HEAD_DIM)
    sin64 = jnp.concatenate([-jnp.sin(ang), jnp.sin(ang)], axis=-1).reshape(m, HEAD_DIM)

    sp = np.cumsum((0, ATTN_WIDTH, ATTN_WIDTH, ATTN_WIDTH, IDX_HEADS * IDX_DIM, IDX_DIM, IDX_HEADS, D_INNER,
                    D_INNER + BC_WIDTH, SSM_HEADS, D_MODEL, D_MODEL))
    col = lambda k: w_in[:, :, sp[k]:sp[k + 1]]
    zpad = lambda n: jnp.zeros((depth, d, n), w_in.dtype)
    w_a = jnp.concatenate([col(0), col(1), col(2), col(3), col(5), zpad(LANES - IDX_DIM - IDX_HEADS), col(4)],
                          axis=-1).astype(BF16)
    w_b = jnp.concatenate([col(6), col(7), col(8), zpad(LANES - SSM_HEADS)], axis=-1).astype(BF16)
    w_c = jnp.concatenate([col(9), col(10)], axis=-1).astype(BF16)
    wa_o, ws_o, wo = w_attn_o.astype(BF16), w_ssm_o.astype(BF16), w_out.astype(BF16)
    wup, wdn = w_up.astype(BF16), w_down.astype(BF16)

    lane_pad = lambda v: jnp.pad(v, ((0, 0), (0, LANES - v.shape[-1])))
    dtb = lane_pad(dt_bias.astype(F32))
    alog = lane_pad(a_log.astype(F32))
    dskip_x = jnp.repeat(d_skip.astype(F32), SSM_HEAD_DIM, axis=-1)
    expand = (jnp.arange(LANES)[:, None] == (jnp.arange(D_INNER)[None, :] // SSM_HEAD_DIM)).astype(BF16)

    c_pad = jnp.pad(c.astype(F32), ((0, SUBLANES - bsz), (0, 0)))
    mod = _adaln(c_pad, w_ada, b_ada)[:, :bsz]

    x2 = x.reshape(m, d)
    for l in range(depth):
        sh1, sc1, g1, sh2, sc2, g2 = [mod[l, :, k * d:(k + 1) * d].reshape(bsz, 1, d) for k in range(6)]
        g_n1 = norm1_g[l].reshape(1, d)
        a_proj = _nm_matmul(x2, g_n1, sc1, sh1, w_a[l], seq)
        b_proj = _nm_matmul(x2, g_n1, sc1, sh1, w_b[l], seq)
        c_proj = _nm_matmul(x2, g_n1, sc1, sh1, w_c[l], seq)

        q_t, k_hm, v_t, iq_t, ik, iw_t = _prep(a_proj, cos64, sin64, q_norm_g[l].reshape(1, HEAD_DIM),
                                               k_norm_g[l].reshape(1, HEAD_DIM), bsz, seq)
        mask_t = _select(iq_t, ik, iw_t, bsz, seq, n_keep)
        o_attn = _attention(q_t, k_hm, v_t, mask_t, bsz, seq)

        y_ssm = _ssd(b_proj, ssm_conv_w[l, :, :D_INNER], ssm_conv_b[l, :D_INNER].reshape(1, D_INNER),
                     ssm_conv_w[l, :, D_INNER:], ssm_conv_b[l, D_INNER:].reshape(1, BC_WIDTH),
                     dtb[l].reshape(1, LANES), alog[l].reshape(1, LANES), dskip_x[l].reshape(1, D_INNER),
                     ssm_norm_g[l].reshape(1, D_INNER), expand, bsz, seq)

        x2 = _merge(x2, o_attn.reshape(m, ATTN_WIDTH), y_ssm.reshape(m, D_INNER), c_proj, g1,
                    wa_o[l], ws_o[l], wo[l], seq)
        x2 = _ffn(x2, norm2_g[l].reshape(1, d), sc2, sh2, g2, wup[l], ffn_conv_w[l],
                  ffn_conv_b[l].reshape(1, 2 * D_FF), wdn[l], bsz, seq)
    return x2.reshape(bsz, seq, d)
```

```python
import functools

import numpy as np
import jax
import jax.numpy as jnp
from jax import lax
from jax.experimental import pallas as pl
from jax.experimental.pallas import tpu as pltpu

F32 = jnp.float32
BF16 = jnp.bfloat16

D_MODEL = 1024
ATTN_HEADS = 16
HEAD_DIM = 64
ATTN_WIDTH = ATTN_HEADS * HEAD_DIM
IDX_HEADS = 8
IDX_DIM = 64
IDX_SCALE = (IDX_HEADS ** -0.5) * (IDX_DIM ** -0.5)
TOPK_MAX = 256
ROPE_THETA = 10000.0
D_INNER = 2 * D_MODEL
SSM_HEAD_DIM = 64
SSM_HEADS = D_INNER // SSM_HEAD_DIM
SSM_GROUPS = 4
HEADS_PER_GROUP = SSM_HEADS // SSM_GROUPS
D_STATE = 128
SSM_CONV = 4
SSM_CHUNK = 256
BC_WIDTH = 2 * SSM_GROUPS * D_STATE
D_FF = 2816
FFN_CONV = 3
EPS = 1e-6

LANES = 128
SUBLANES = 8
VMEM_LIMIT = 56 * 1024 * 1024

A_WIDTH = 3 * ATTN_WIDTH + IDX_HEADS * IDX_DIM + LANES
B_WIDTH = 2 * D_INNER + BC_WIDTH + LANES
C_WIDTH = 2 * D_MODEL

NEG_BIG = -1e30
INT_MIN = -2 ** 31
KEY_NEG_INF = -0x7F800000
S_PARTS = 4


def _float_bits_to_key(bits):
    return jnp.where(bits >= 0, bits, -(bits & 0x7FFFFFFF))


def _cparams(n_axes):
    return pltpu.CompilerParams(dimension_semantics=("arbitrary",) * n_axes,
                                vmem_limit_bytes=VMEM_LIMIT)


def _sigmoid(x):
    return 1.0 / (1.0 + jnp.exp(-x))


def _silu(x):
    return x * _sigmoid(x)


def _dot(a, b):
    return jnp.dot(a, b, preferred_element_type=F32)


def _dot_nt(a, b):
    return lax.dot_general(a, b, (((1,), (1,)), ((), ())), preferred_element_type=F32)


def _split3(a):
    hi = a.astype(BF16)
    r1 = a - hi.astype(F32)
    mid = r1.astype(BF16)
    lo = (r1 - mid.astype(F32)).astype(BF16)
    return hi, mid, lo


def _dot3(a, b01):
    hi, mid, lo = _split3(a)
    return _dot(hi, b01) + _dot(mid, b01) + _dot(lo, b01)


def _dot3_left(b01, a):
    hi, mid, lo = _split3(a)
    return _dot(b01, hi) + _dot(b01, mid) + _dot(b01, lo)


def _adaln_kernel(c_ref, w_ref, b_ref, o_ref):
    c = c_ref[...]
    act = _silu(c).astype(BF16)
    o_ref[0] = _dot(act, w_ref[0].astype(BF16)) + b_ref[0]


def _adaln(c_pad, w_ada, b_ada):
    depth, d, n = w_ada.shape
    tn = 1536
    return pl.pallas_call(
        _adaln_kernel,
        grid=(depth, n // tn),
        in_specs=[pl.BlockSpec((SUBLANES, d), lambda l, j: (0, 0)),
                  pl.BlockSpec((1, d, tn), lambda l, j: (l, 0, j)),
                  pl.BlockSpec((1, 1, tn), lambda l, j: (l, 0, j))],
        out_specs=pl.BlockSpec((1, SUBLANES, tn), lambda l, j: (l, 0, j)),
        out_shape=jax.ShapeDtypeStruct((depth, SUBLANES, n), F32),
        compiler_params=_cparams(2),
    )(c_pad, w_ada, b_ada.reshape(depth, 1, n))


def _norm_mod(x, g, sc, sh):
    ms = jnp.mean(x * x, axis=-1, keepdims=True)
    h = x * lax.rsqrt(ms + EPS) * g
    return h * (1.0 + sc) + sh


def _nm_matmul_kernel(x_ref, g_ref, sc_ref, sh_ref, w_ref, o_ref, *, chunks):
    h = _norm_mod(x_ref[...], g_ref[...], sc_ref[0], sh_ref[0]).astype(BF16)
    for (c0, c1) in chunks:
        o_ref[:, c0:c1] = _dot(h, w_ref[:, c0:c1]).astype(o_ref.dtype)


def _col_chunks(n, step=1024):
    out, c = [], 0
    while c < n:
        out.append((c, min(c + step, n)))
        c += step
    return tuple(out)


def _nm_matmul(x2, g, sc, sh, w, seq, tm=256):
    m, d = x2.shape
    n = w.shape[1]
    tpb = seq // tm
    return pl.pallas_call(
        functools.partial(_nm_matmul_kernel, chunks=_col_chunks(n)),
        grid=(m // tm,),
        in_specs=[pl.BlockSpec((tm, d), lambda i: (i, 0)),
                  pl.BlockSpec((1, d), lambda i: (0, 0)),
                  pl.BlockSpec((1, 1, d), lambda i: (i // tpb, 0, 0)),
                  pl.BlockSpec((1, 1, d), lambda i: (i // tpb, 0, 0)),
                  pl.BlockSpec((d, n), lambda i: (0, 0))],
        out_specs=pl.BlockSpec((tm, n), lambda i: (i, 0)),
        out_shape=jax.ShapeDtypeStruct((m, n), F32),
        compiler_params=_cparams(1),
    )(x2, g, sc, sh, w)


def _prep_kernel(x_ref, g_ref, sc_ref, sh_ref, w_ref, cos_ref, sin_ref, qg_ref, kg_ref,
                 qo_ref, ko_ref, vo_ref, iqo_ref, iko_ref, iwo_ref):
    tm = x_ref.shape[0]
    half = HEAD_DIM // 2
    cos = cos_ref[...]
    sins = sin_ref[...]
    lane = lax.broadcasted_iota(jnp.int32, (tm, LANES), 1)
    first_half = (lane & (HEAD_DIM - 1)) < half
    r_io = lax.broadcasted_iota(jnp.int32, (LANES, LANES), 0)
    c_io = lax.broadcasted_iota(jnp.int32, (LANES, LANES), 1)
    same_head = jnp.where((r_io < HEAD_DIM) == (c_io < HEAD_DIM), 1.0, 0.0).astype(BF16)

    def rope(x):
        partner = jnp.where(first_half, pltpu.roll(x, LANES - half, 1), pltpu.roll(x, half, 1))
        return x * cos + partner * sins

    def head_norm(x, g):
        sq = x * x
        hi = sq.astype(BF16)
        lo = (sq - hi.astype(F32)).astype(BF16)
        ss = _dot(hi, same_head) + _dot(lo, same_head)
        return x * lax.rsqrt(ss * (1.0 / HEAD_DIM) + EPS) * g

    h = _norm_mod(x_ref[...], g_ref[...], sc_ref[0], sh_ref[0]).astype(BF16)

    def proj(c0, width):
        return _dot(h, w_ref[:, c0:c0 + width])

    qg = qg_ref[...]
    kg = kg_ref[...]
    zeros = jnp.zeros((HEAD_DIM, tm), BF16)
    wide = 2 * LANES
    k_col, v_col, iq_col = ATTN_WIDTH, 2 * ATTN_WIDTH, 3 * ATTN_WIDTH
    misc_col = iq_col + IDX_HEADS * IDX_DIM
    for t2 in range(ATTN_WIDTH // wide):
        qq = proj(t2 * wide, wide)
        kk = proj(k_col + t2 * wide, wide)
        for u in range(2):
            t = 2 * t2 + u
            lanes = slice(u * LANES, (u + 1) * LANES)
            q_t = (rope(head_norm(qq[:, lanes], qg)) * (HEAD_DIM ** -0.5)).T.astype(BF16)
            qo_ref[0, 2 * t, 0:HEAD_DIM, :] = q_t[0:HEAD_DIM]
            qo_ref[0, 2 * t, HEAD_DIM:LANES, :] = zeros
            qo_ref[0, 2 * t + 1, 0:HEAD_DIM, :] = zeros
            qo_ref[0, 2 * t + 1, HEAD_DIM:LANES, :] = q_t[HEAD_DIM:LANES]
            ko_ref[0, t] = rope(head_norm(kk[:, lanes], kg)).astype(BF16)
        vo_ref[0, t2 * wide:(t2 + 1) * wide, :] = proj(v_col + t2 * wide, wide).T.astype(BF16)
    for t2 in range(IDX_HEADS * IDX_DIM // wide):
        pp = proj(iq_col + t2 * wide, wide)
        for u in range(2):
            rows = slice((2 * t2 + u) * LANES, (2 * t2 + u + 1) * LANES)
            iqo_ref[0, rows, :] = rope(pp[:, u * LANES:(u + 1) * LANES]).T.astype(BF16)
    misc = proj(misc_col, LANES)
    iko_ref[0] = rope(misc)[:, IDX_DIM:2 * IDX_DIM].astype(BF16)
    iwo_ref[0] = misc.T[0:IDX_HEADS, :]


def _prep(x2, g, sc, sh, w, cos64, sin64, qg, kg, bsz, seq, tm=256):
    tpb = seq // tm
    row = lambda b, i: b * tpb + i
    d = x2.shape[1]
    aw = ATTN_WIDTH
    iqw = IDX_HEADS * IDX_DIM
    return pl.pallas_call(
        _prep_kernel,
        grid=(bsz, tpb),
        in_specs=[pl.BlockSpec((tm, d), lambda b, i: (row(b, i), 0)),
                  pl.BlockSpec((1, d), lambda b, i: (0, 0)),
                  pl.BlockSpec((1, 1, d), lambda b, i: (b, 0, 0)),
                  pl.BlockSpec((1, 1, d), lambda b, i: (b, 0, 0)),
                  pl.BlockSpec((d, A_WIDTH), lambda b, i: (0, 0)),
                  pl.BlockSpec((tm, LANES), lambda b, i: (row(b, i), 0)),
                  pl.BlockSpec((tm, LANES), lambda b, i: (row(b, i), 0)),
                  pl.BlockSpec((1, LANES), lambda b, i: (0, 0)),
                  pl.BlockSpec((1, LANES), lambda b, i: (0, 0))],
        out_specs=[pl.BlockSpec((1, ATTN_HEADS, LANES, tm), lambda b, i: (b, 0, 0, i)),
                   pl.BlockSpec((1, aw // LANES, tm, LANES), lambda b, i: (b, 0, i, 0)),
                   pl.BlockSpec((1, aw, tm), lambda b, i: (b, 0, i)),
                   pl.BlockSpec((1, iqw, tm), lambda b, i: (b, 0, i)),
                   pl.BlockSpec((1, tm, IDX_DIM), lambda b, i: (b, i, 0)),
                   pl.BlockSpec((1, IDX_HEADS, tm), lambda b, i: (b, 0, i))],
        out_shape=[jax.ShapeDtypeStruct((bsz, ATTN_HEADS, LANES, seq), BF16),
                   jax.ShapeDtypeStruct((bsz, aw // LANES, seq, LANES), BF16),
                   jax.ShapeDtypeStruct((bsz, aw, seq), BF16),
                   jax.ShapeDtypeStruct((bsz, iqw, seq), BF16),
                   jax.ShapeDtypeStruct((bsz, seq, IDX_DIM), BF16),
                   jax.ShapeDtypeStruct((bsz, IDX_HEADS, seq), F32)],
        compiler_params=_cparams(2),
    )(x2, g, sc, sh, w, cos64, sin64, qg, kg)


def _select_kernel(iq_ref, ik_ref, iw_ref, mask_ref, key_ref, *, tq, ck, n_keep, seq):
    i = pl.program_id(1)
    n_ch = ((i + 1) * tq + ck - 1) // ck
    w = iw_ref[0]
    q_pos = i * tq + lax.broadcasted_iota(jnp.int32, (ck, tq), 1)
    k_iota = lax.broadcasted_iota(jnp.int32, (ck, tq), 0)

    def score_chunk(c, carry):
        off = pl.multiple_of(c * ck, ck)
        kt = ik_ref[0, pl.ds(off, ck), :]
        acc = jnp.zeros((ck, tq), F32)
        for h in range(IDX_HEADS):
            logit = _dot(kt, iq_ref[0, h * IDX_DIM:(h + 1) * IDX_DIM, :])
            acc = acc + w[h:h + 1, :] * jnp.maximum(logit, 0.0)
        score = acc * IDX_SCALE
        key = _float_bits_to_key(pltpu.bitcast(score, jnp.int32))
        key_ref[pl.ds(off, ck), :] = jnp.where(k_iota + off <= q_pos, key, KEY_NEG_INF)
        return carry

    lax.fori_loop(0, n_ch, score_chunk, 0)

    cnt_rows = 4 * SUBLANES

    def count_ge(cand):
        def body(c, acc):
            off = pl.multiple_of(c * ck, ck)
            hit = jnp.where(key_ref[pl.ds(off, ck), :] >= cand, 1.0, 0.0)
            return acc + jnp.sum(hit.reshape(ck // cnt_rows, cnt_rows, tq), axis=0)
        acc = lax.fori_loop(0, n_ch, body, jnp.zeros((cnt_rows, tq), F32))
        return jnp.sum(acc, axis=0, keepdims=True)

    kf = jnp.float32(n_keep)

    def bit_step(it, state):
        t_u, c_ge, c_gt = state
        bit = lax.shift_left(jnp.int32(1), 31 - it)
        cand_u = t_u | bit
        cnt = count_ge(cand_u ^ INT_MIN)
        keep = cnt >= kf
        return jnp.where(keep, cand_u, t_u), jnp.where(keep, cnt, c_ge), jnp.where(keep, c_gt, cnt)

    n_all = (n_ch * ck).astype(F32) + jnp.zeros((1, tq), F32)
    t_u, cnt_ge, cnt_gt = lax.fori_loop(0, 32, bit_step, (jnp.zeros((1, tq), jnp.int32), n_all,
                                                          jnp.zeros((1, tq), F32)))
    thr = t_u ^ INT_MIN
    need = kf - cnt_gt
    tie = jnp.where((cnt_ge > kf) & (thr > KEY_NEG_INF), 1, 0)
    any_tie = jnp.max(tie) > 0
    thr_eff = jnp.maximum(thr, KEY_NEG_INF + 1)

    @pl.when(jnp.logical_not(any_tie))
    def _():
        def body(c, carry):
            off = pl.multiple_of(c * ck, ck)
            kc = key_ref[pl.ds(off, ck), :]
            mask_ref[0, pl.ds(off, ck), :] = jnp.where(kc >= thr_eff, 1, 0).astype(jnp.int8)
            return carry
        lax.fori_loop(0, n_ch, body, 0)

    @pl.when(any_tie)
    def _():
        lower = jnp.where(lax.broadcasted_iota(jnp.int32, (ck, ck), 1)
                          < lax.broadcasted_iota(jnp.int32, (ck, ck), 0), 1.0, 0.0).astype(BF16)

        def body(c, seen):
            off = pl.multiple_of(c * ck, ck)
            kc = key_ref[pl.ds(off, ck), :]
            eq = jnp.where(kc == thr, 1.0, 0.0)
            rank = seen + _dot(lower, eq.astype(BF16))
            take_eq = jnp.where(rank < need, eq, 0.0)
            sel = jnp.where(kc > thr, 1.0, take_eq)
            sel = jnp.where(kc > KEY_NEG_INF, sel, 0.0)
            mask_ref[0, pl.ds(off, ck), :] = sel.astype(jnp.int32).astype(jnp.int8)
            return seen + jnp.sum(eq, axis=0, keepdims=True)
        lax.fori_loop(0, n_ch, body, jnp.zeros((1, tq), F32))

    def zero_tail(c, carry):
        off = pl.multiple_of(c * ck, ck)
        mask_ref[0, pl.ds(off, ck), :] = jnp.zeros((ck, tq), jnp.int8)
        return carry
    lax.fori_loop(n_ch, seq // ck, zero_tail, 0)


def _select(iq_t, ik, iw_t, bsz, seq, n_keep, tq=256, ck=512):
    ck = min(ck, seq)
    return pl.pallas_call(
        functools.partial(_select_kernel, tq=tq, ck=ck, n_keep=n_keep, seq=seq),
        grid=(bsz, seq // tq),
        in_specs=[pl.BlockSpec((1, IDX_HEADS * IDX_DIM, tq), lambda b, i: (b, 0, i)),
                  pl.BlockSpec((1, seq, IDX_DIM), lambda b, i: (b, 0, 0)),
                  pl.BlockSpec((1, IDX_HEADS, tq), lambda b, i: (b, 0, i))],
        out_specs=pl.BlockSpec((1, seq, tq), lambda b, i: (b, 0, i)),
        out_shape=jax.ShapeDtypeStruct((bsz, seq, seq), jnp.int8),
        scratch_shapes=[pltpu.VMEM((seq, tq), jnp.int32)],
        compiler_params=_cparams(2),
    )(iq_t, ik, iw_t)


def _attn_kernel(q_ref, k_ref, v_ref, mask_ref, o_ref, m_ref, l_ref, acc_ref, bias_ref, s0_ref, s1_ref, *, tq, tk):
    i = pl.program_id(1)
    j = pl.program_id(2)
    last_j = ((i + 1) * tq - 1) // tk

    @pl.when(j == 0)
    def _():
        m_ref[...] = jnp.full(m_ref.shape, NEG_BIG, F32)
        l_ref[...] = jnp.zeros(l_ref.shape, F32)
        acc_ref[...] = jnp.zeros(acc_ref.shape, F32)

    @pl.when(j <= last_j)
    def _():
        bias_ref[...] = jnp.where(mask_ref[0].astype(jnp.int32) != 0, 0.0, NEG_BIG)

        def scores(h, s_ref):
            qh = q_ref[0, h]
            pair = h // 2
            for part in range(S_PARTS):
                ks = slice(part * (tk // S_PARTS), (part + 1) * (tk // S_PARTS))
                s_ref[ks, :] = _dot(k_ref[0, pair, ks, :], qh) + bias_ref[ks, :]

        def softmax_pv(h, s_ref):
            rows = pl.ds(pl.multiple_of(h * HEAD_DIM, HEAD_DIM), HEAD_DIM)
            s = s_ref[...]
            m_old = m_ref[pl.ds(h, 1), :]
            m_new = jnp.maximum(m_old, jnp.max(s, axis=0, keepdims=True))
            alpha = jnp.exp(m_old - m_new)
            p = jnp.exp(s - m_new)
            l_ref[pl.ds(h, 1), :] = alpha * l_ref[pl.ds(h, 1), :] + jnp.sum(p, axis=0, keepdims=True)
            acc_ref[rows, :] = alpha * acc_ref[rows, :] + _dot(v_ref[0, rows, :], p.astype(BF16))
            m_ref[pl.ds(h, 1), :] = m_new

        scores(0, s0_ref)

        def head_pair(g, carry):
            h = 2 * g
            scores(h + 1, s1_ref)
            softmax_pv(h, s0_ref)
            scores(jnp.minimum(h + 2, ATTN_HEADS - 1), s0_ref)
            softmax_pv(h + 1, s1_ref)
            return carry
        lax.fori_loop(0, ATTN_HEADS // 2, head_pair, 0)

    @pl.when(j == last_j)
    def _():
        for h in range(ATTN_HEADS):
            rows = slice(h * HEAD_DIM, (h + 1) * HEAD_DIM)
            acc_ref[rows, :] = acc_ref[rows, :] / l_ref[h:h + 1, :]
        o_ref[0] = acc_ref[...].T.astype(o_ref.dtype)


def _attention(q_t, k_hm, v_t, mask_t, bsz, seq, tq=256, tk=512):
    tk = min(tk, seq)
    tq = min(tq, tk)
    nq, nk = seq // tq, seq // tk
    last = lambda i: ((i + 1) * tq - 1) // tk
    kj = lambda i, j: jnp.minimum(j, last(i))
    return pl.pallas_call(
        functools.partial(_attn_kernel, tq=tq, tk=tk),
        grid=(bsz, nq, nk),
        in_specs=[pl.BlockSpec((1, ATTN_HEADS, LANES, tq), lambda b, i, j: (b, 0, 0, i)),
                  pl.BlockSpec((1, ATTN_WIDTH // LANES, tk, LANES), lambda b, i, j: (b, 0, kj(i, j), 0)),
                  pl.BlockSpec((1, ATTN_WIDTH, tk), lambda b, i, j: (b, 0, kj(i, j))),
                  pl.BlockSpec((1, tk, tq), lambda b, i, j: (b, kj(i, j), i))],
        out_specs=pl.BlockSpec((1, tq, ATTN_WIDTH), lambda b, i, j: (b, i, 0)),
        out_shape=jax.ShapeDtypeStruct((bsz, seq, ATTN_WIDTH), BF16),
        scratch_shapes=[pltpu.VMEM((ATTN_HEADS, tq), F32),
                        pltpu.VMEM((ATTN_HEADS, tq), F32),
                        pltpu.VMEM((ATTN_WIDTH, tq), F32),
                        pltpu.VMEM((tk, tq), F32),
                        pltpu.VMEM((tk, tq), F32),
                        pltpu.VMEM((tk, tq), F32)],
        compiler_params=_cparams(3),
    )(q_t, k_hm, v_t, mask_t)


def _ssd_kernel(z_ref, xs_ref, bc_ref, dt_ref, cwx_ref, cbx_ref, cwb_ref, cbb_ref, dtb_ref, alog_ref,
                dskip_ref, gain_ref, expand_ref, o_ref, extx_ref, extb_ref, state_ref, y_ref, *, t):
    c = pl.program_id(1)
    halo = SUBLANES

    @pl.when(c == 0)
    def _():
        extx_ref[0:halo, :] = jnp.zeros((halo, D_INNER), F32)
        extb_ref[0:halo, :] = jnp.zeros((halo, BC_WIDTH), F32)
        state_ref[...] = jnp.zeros(state_ref.shape, F32)

    def conv_silu(ext_ref, raw_ref, w_ref, b_ref):
        ext_ref[halo:halo + t, :] = raw_ref[...]
        y = b_ref[...] + w_ref[SSM_CONV - 1:SSM_CONV, :] * ext_ref[halo:halo + t, :]
        for k in range(SSM_CONV - 1):
            shift = SSM_CONV - 1 - k
            y = y + w_ref[k:k + 1, :] * ext_ref[halo - shift:halo - shift + t, :]
        ext_ref[0:halo, :] = ext_ref[t:t + halo, :]
        return _silu(y)

    xs = conv_silu(extx_ref, xs_ref, cwx_ref, cbx_ref)
    bc = conv_silu(extb_ref, bc_ref, cwb_ref, cbb_ref)

    dtv = dt_ref[...] + dtb_ref[...]
    dt = jnp.maximum(dtv, 0.0) + jnp.log1p(jnp.exp(-jnp.abs(dtv)))
    a = dt * (-jnp.exp(alog_ref[...]))

    r_io = lax.broadcasted_iota(jnp.int32, (t, t), 0)
    c_io = lax.broadcasted_iota(jnp.int32, (t, t), 1)
    tri = r_io >= c_io
    upper_incl = jnp.where(r_io <= c_io, 1.0, 0.0).astype(BF16)
    lower_incl = jnp.where(tri, 1.0, 0.0).astype(BF16)
    a_cs = _dot3_left(lower_incl, a)
    a_cs_t = _dot3(a.T, upper_incl)
    expand = expand_ref[...]
    dt_x = _dot3(dt, expand)
    acs_x = _dot3(a_cs, expand)
    alast_x = acs_x[t - 1:t, :]
    xdt = xs * dt_x
    e_start = jnp.exp(acs_x)
    xw = (xdt * jnp.exp(alast_x - acs_x)).astype(BF16)
    xdt_b = xdt.astype(BF16)
    chunk_decay = jnp.exp(alast_x)
    left_half = lax.broadcasted_iota(jnp.int32, (t, LANES), 1) < SSM_HEAD_DIM

    gw = SSM_GROUPS * D_STATE
    for g in range(SSM_GROUPS):
        bm = bc[:, g * D_STATE:(g + 1) * D_STATE]
        cm = bc[:, gw + g * D_STATE:gw + (g + 1) * D_STATE].astype(BF16)
        bm_t = bm.T.astype(BF16)
        cb = _dot_nt(cm, bm.astype(BF16))
        for p in range(HEADS_PER_GROUP // 2):
            pair = g * (HEADS_PER_GROUP // 2) + p
            lanes = slice(pair * LANES, (pair + 1) * LANES)
            xp = xdt_b[:, lanes]
            ys = []
            for e in range(2):
                h = 2 * pair + e
                seg = a_cs[:, h:h + 1] - a_cs_t[h:h + 1, :]
                dec = jnp.exp(jnp.where(tri, seg, -jnp.inf))
                ys.append(_dot((cb * dec).astype(BF16), xp))
            y_diag = jnp.where(left_half, ys[0], ys[1])
            st = state_ref[pair]
            y_off = _dot(cm, st.astype(BF16)) * e_start[:, lanes]
            y_ref[:, lanes] = y_diag + y_off
            state_ref[pair] = st * chunk_decay[:, lanes] + _dot(bm_t, xw[:, lanes])

    y = (y_ref[...] + xs * dskip_ref[...]) * _silu(z_ref[...])
    gwid = D_INNER // SSM_GROUPS
    for g in range(SSM_GROUPS):
        yg = y[:, g * gwid:(g + 1) * gwid]
        ms = jnp.mean(yg * yg, axis=-1, keepdims=True)
        o_ref[0, :, g * gwid:(g + 1) * gwid] = (yg * lax.rsqrt(ms + EPS)
                                                 * gain_ref[:, g * gwid:(g + 1) * gwid]).astype(o_ref.dtype)


def _ssd(b_proj, cwx, cbx, cwb, cbb, dtb, alog, dskip_x, gain, expand, bsz, seq):
    t = SSM_CHUNK
    nc = seq // t
    row = lambda b, c: b * nc + c
    const = lambda shape: pl.BlockSpec(shape, lambda b, c: (0,) * len(shape))
    return pl.pallas_call(
        functools.partial(_ssd_kernel, t=t),
        grid=(bsz, nc),
        in_specs=[pl.BlockSpec((t, D_INNER), lambda b, c: (row(b, c), 0)),
                  pl.BlockSpec((t, D_INNER), lambda b, c: (row(b, c), 1)),
                  pl.BlockSpec((t, BC_WIDTH), lambda b, c: (row(b, c), 2 * D_INNER // BC_WIDTH)),
                  pl.BlockSpec((t, LANES), lambda b, c: (row(b, c), (2 * D_INNER + BC_WIDTH) // LANES)),
                  const((SSM_CONV, D_INNER)), const((1, D_INNER)),
                  const((SSM_CONV, BC_WIDTH)), const((1, BC_WIDTH)),
                  const((1, LANES)), const((1, LANES)),
                  const((1, D_INNER)), const((1, D_INNER)),
                  const((LANES, D_INNER))],
        out_specs=pl.BlockSpec((1, t, D_INNER), lambda b, c: (b, c, 0)),
        out_shape=jax.ShapeDtypeStruct((bsz, seq, D_INNER), BF16),
        scratch_shapes=[pltpu.VMEM((t + SUBLANES, D_INNER), F32),
                        pltpu.VMEM((t + SUBLANES, BC_WIDTH), F32),
                        pltpu.VMEM((SSM_HEADS // 2, D_STATE, LANES), F32),
                        pltpu.VMEM((t, D_INNER), F32)],
        compiler_params=_cparams(2),
    )(b_proj, b_proj, b_proj, b_proj, cwx, cbx, cwb, cbb, dtb, alog, dskip_x, gain, expand)


def _merge_kernel(x_ref, oa_ref, ys_ref, ga_ref, gm_ref, g1_ref, wa_ref, ws_ref, wo_ref, o_ref):
    o_attn = _dot(oa_ref[...], wa_ref[...])
    o_ssm = _dot(ys_ref[...], ws_ref[...])
    mix = _sigmoid(ga_ref[...]) * o_attn + _sigmoid(gm_ref[...]) * o_ssm
    o_ref[...] = x_ref[...] + g1_ref[0] * _dot(mix.astype(BF16), wo_ref[...])


def _merge(x2, oa2, ys2, c_proj, g1, wa, ws, wo, seq, tm=256):
    m, d = x2.shape
    tpb = seq // tm
    const = lambda shape: pl.BlockSpec(shape, lambda i: (0,) * len(shape))
    return pl.pallas_call(
        _merge_kernel,
        grid=(m // tm,),
        in_specs=[pl.BlockSpec((tm, d), lambda i: (i, 0)),
                  pl.BlockSpec((tm, ATTN_WIDTH), lambda i: (i, 0)),
                  pl.BlockSpec((tm, D_INNER), lambda i: (i, 0)),
                  pl.BlockSpec((tm, d), lambda i: (i, 0)),
                  pl.BlockSpec((tm, d), lambda i: (i, 1)),
                  pl.BlockSpec((1, 1, d), lambda i: (i // tpb, 0, 0)),
                  const((ATTN_WIDTH, d)), const((D_INNER, d)), const((d, d))],
        out_specs=pl.BlockSpec((tm, d), lambda i: (i, 0)),
        out_shape=jax.ShapeDtypeStruct((m, d), F32),
        compiler_params=_cparams(1),
    )(x2, oa2, ys2, c_proj, c_proj, g1, wa, ws, wo)


def _ffn_kernel(x_ref, g_ref, sc_ref, sh_ref, g2_ref, wup_ref, cw_ref, cb_ref, wdn_ref, o_ref, ext_ref, *, tm):
    i = pl.program_id(1)
    halo = SUBLANES

    @pl.when(i == 0)
    def _():
        ext_ref[0:halo, :] = jnp.zeros((halo, 2 * D_FF), F32)

    x = x_ref[...]
    h = _norm_mod(x, g_ref[...], sc_ref[0], sh_ref[0]).astype(BF16)
    for (c0, c1) in _col_chunks(2 * D_FF, 1408):
        ext_ref[halo:halo + tm, c0:c1] = _dot(h, wup_ref[:, c0:c1])

    def conv(c0, c1):
        u = cb_ref[:, c0:c1] + cw_ref[FFN_CONV - 1:FFN_CONV, c0:c1] * ext_ref[halo:halo + tm, c0:c1]
        for k in range(FFN_CONV - 1):
            shift = FFN_CONV - 1 - k
            u = u + cw_ref[k:k + 1, c0:c1] * ext_ref[halo - shift:halo - shift + tm, c0:c1]
        return u

    out = jnp.zeros((tm, D_MODEL), F32)
    for (c0, c1) in _col_chunks(D_FF, 1408):
        act = _silu(conv(D_FF + c0, D_FF + c1)) * conv(c0, c1)
        out = out + _dot(act.astype(BF16), wdn_ref[c0:c1, :])
    o_ref[...] = x + g2_ref[0] * out
    ext_ref[0:halo, :] = ext_ref[tm:tm + halo, :]


def _ffn(x2, g, sc, sh, g2, wup, cw, cb, wdn, bsz, seq, tm=256):
    m, d = x2.shape
    tpb = seq // tm
    row = lambda b, i: b * tpb + i
    const = lambda shape: pl.BlockSpec(shape, lambda b, i: (0,) * len(shape))
    return pl.pallas_call(
        functools.partial(_ffn_kernel, tm=tm),
        grid=(bsz, tpb),
        in_specs=[pl.BlockSpec((tm, d), lambda b, i: (row(b, i), 0)),
                  const((1, d)),
                  pl.BlockSpec((1, 1, d), lambda b, i: (b, 0, 0)),
                  pl.BlockSpec((1, 1, d), lambda b, i: (b, 0, 0)),
                  pl.BlockSpec((1, 1, d), lambda b, i: (b, 0, 0)),
                  const((d, 2 * D_FF)), const((FFN_CONV, 2 * D_FF)), const((1, 2 * D_FF)),
                  const((D_FF, d))],
        out_specs=pl.BlockSpec((tm, d), lambda b, i: (row(b, i), 0)),
        out_shape=jax.ShapeDtypeStruct((m, d), F32),
        scratch_shapes=[pltpu.VMEM((tm + SUBLANES, 2 * D_FF), F32)],
        compiler_params=_cparams(2),
    )(x2, g, sc, sh, g2, wup, cw, cb, wdn)


def kernel(x, c, positions, w_ada, b_ada, norm1_g, w_in, q_norm_g, k_norm_g, ssm_conv_w, ssm_conv_b, dt_bias,
           a_log, d_skip, ssm_norm_g, w_attn_o, w_ssm_o, w_out, norm2_g, w_up, ffn_conv_w, ffn_conv_b, w_down):
    bsz, seq, d = x.shape
    depth = w_ada.shape[0]
    n_keep = min(TOPK_MAX, seq // 4)
    m = bsz * seq

    inv = 1.0 / (ROPE_THETA ** (jnp.arange(0, HEAD_DIM, 2, dtype=F32) / HEAD_DIM))
    ang = positions.astype(F32)[..., None] * inv
    cos64 = jnp.concatenate([jnp.cos(ang)] * 4, axis=-1).reshape(m, LANES)
    sin64 = jnp.concatenate([-jnp.sin(ang), jnp.sin(ang)] * 2, axis=-1).reshape(m, LANES)

    sp = np.cumsum((0, ATTN_WIDTH, ATTN_WIDTH, ATTN_WIDTH, IDX_HEADS * IDX_DIM, IDX_DIM, IDX_HEADS, D_INNER,
                    D_INNER + BC_WIDTH, SSM_HEADS, D_MODEL, D_MODEL))
    col = lambda k: w_in[:, :, sp[k]:sp[k + 1]]
    zpad = lambda n: jnp.zeros((depth, d, n), w_in.dtype)
    w_a = jnp.concatenate([col(0), col(1), col(2), col(3), col(5), zpad(LANES - IDX_DIM - IDX_HEADS), col(4)],
                          axis=-1).astype(BF16)
    w_b = jnp.concatenate([col(6), col(7), col(8), zpad(LANES - SSM_HEADS)], axis=-1).astype(BF16)
    w_c = jnp.concatenate([col(9), col(10)], axis=-1).astype(BF16)
    wa_o, ws_o, wo = w_attn_o.astype(BF16), w_ssm_o.astype(BF16), w_out.astype(BF16)
    wup, wdn = w_up.astype(BF16), w_down.astype(BF16)

    lane_pad = lambda v: jnp.pad(v, ((0, 0), (0, LANES - v.shape[-1])))
    dtb = lane_pad(dt_bias.astype(F32))
    alog = lane_pad(a_log.astype(F32))
    dskip_x = jnp.repeat(d_skip.astype(F32), SSM_HEAD_DIM, axis=-1)
    expand = (jnp.arange(LANES)[:, None] == (jnp.arange(D_INNER)[None, :] // SSM_HEAD_DIM)).astype(BF16)

    c_pad = jnp.pad(c.astype(F32), ((0, SUBLANES - bsz), (0, 0)))
    mod = _adaln(c_pad, w_ada, b_ada)[:, :bsz]

    x2 = x.reshape(m, d)
    for l in range(depth):
        sh1, sc1, g1, sh2, sc2, g2 = [mod[l, :, k * d:(k + 1) * d].reshape(bsz, 1, d) for k in range(6)]
        g_n1 = norm1_g[l].reshape(1, d)
        b_proj = _nm_matmul(x2, g_n1, sc1, sh1, w_b[l], seq)
        c_proj = _nm_matmul(x2, g_n1, sc1, sh1, w_c[l], seq)

        q_t, k_hm, v_t, iq_t, ik, iw_t = _prep(x2, g_n1, sc1, sh1, w_a[l], cos64, sin64,
                                               jnp.tile(q_norm_g[l], 2).reshape(1, LANES),
                                               jnp.tile(k_norm_g[l], 2).reshape(1, LANES), bsz, seq)
        mask_t = _select(iq_t, ik, iw_t, bsz, seq, n_keep)
        o_attn = _attention(q_t, k_hm, v_t, mask_t, bsz, seq)

        y_ssm = _ssd(b_proj, ssm_conv_w[l, :, :D_INNER], ssm_conv_b[l, :D_INNER].reshape(1, D_INNER),
                     ssm_conv_w[l, :, D_INNER:], ssm_conv_b[l, D_INNER:].reshape(1, BC_WIDTH),
                     dtb[l].reshape(1, LANES), alog[l].reshape(1, LANES), dskip_x[l].reshape(1, D_INNER),
                     ssm_norm_g[l].reshape(1, D_INNER), expand, bsz, seq)

        x2 = _merge(x2, o_attn.reshape(m, ATTN_WIDTH), y_ssm.reshape(m, D_INNER), c_proj, g1,
                    wa_o[l], ws_o[l], wo[l], seq)
        x2 = _ffn(x2, norm2_g[l].reshape(1, d), sc2, sh2, g2, wup[l], ffn_conv_w[l],
                  ffn_conv_b[l].reshape(1, 2 * D_FF), wdn[l], bsz, seq)
    return x2.reshape(bsz, seq, d)
```

```python
import functools

import numpy as np
import jax
import jax.numpy as jnp
from jax import lax
from jax.experimental import pallas as pl
from jax.experimental.pallas import tpu as pltpu

F32 = jnp.float32
BF16 = jnp.bfloat16

D_MODEL = 1024
ATTN_HEADS = 16
HEAD_DIM = 64
ATTN_WIDTH = ATTN_HEADS * HEAD_DIM
IDX_HEADS = 8
IDX_DIM = 64
IDX_SCALE = (IDX_HEADS ** -0.5) * (IDX_DIM ** -0.5)
TOPK_MAX = 256
ROPE_THETA = 10000.0
D_INNER = 2 * D_MODEL
SSM_HEAD_DIM = 64
SSM_HEADS = D_INNER // SSM_HEAD_DIM
SSM_GROUPS = 4
HEADS_PER_GROUP = SSM_HEADS // SSM_GROUPS
D_STATE = 128
SSM_CONV = 4
SSM_CHUNK = 256
BC_WIDTH = 2 * SSM_GROUPS * D_STATE
D_FF = 2816
FFN_CONV = 3
EPS = 1e-6

LANES = 128
SUBLANES = 8
VMEM_LIMIT = 56 * 1024 * 1024

A_WIDTH = 3 * ATTN_WIDTH + IDX_HEADS * IDX_DIM + LANES
B_WIDTH = 2 * D_INNER + BC_WIDTH + LANES
C_WIDTH = 2 * D_MODEL

NEG_BIG = -1e30
INT_MIN = -2 ** 31
KEY_NEG_INF = -0x7F800000
S_PARTS = 4


def _float_bits_to_key(bits):
    return jnp.where(bits >= 0, bits, -(bits & 0x7FFFFFFF))


def _cparams(n_axes):
    return pltpu.CompilerParams(dimension_semantics=("arbitrary",) * n_axes,
                                vmem_limit_bytes=VMEM_LIMIT)


def _sigmoid(x):
    return 1.0 / (1.0 + jnp.exp(-x))


def _silu(x):
    return x * _sigmoid(x)


def _dot(a, b):
    return jnp.dot(a, b, preferred_element_type=F32)


def _dot_nt(a, b):
    return lax.dot_general(a, b, (((1,), (1,)), ((), ())), preferred_element_type=F32)


def _split3(a):
    hi = a.astype(BF16)
    r1 = a - hi.astype(F32)
    mid = r1.astype(BF16)
    lo = (r1 - mid.astype(F32)).astype(BF16)
    return hi, mid, lo


def _dot3(a, b01):
    hi, mid, lo = _split3(a)
    return _dot(hi, b01) + _dot(mid, b01) + _dot(lo, b01)


def _dot3_left(b01, a):
    hi, mid, lo = _split3(a)
    return _dot(b01, hi) + _dot(b01, mid) + _dot(b01, lo)


def _adaln_kernel(c_ref, w_ref, b_ref, o_ref):
    c = c_ref[...]
    act = _silu(c).astype(BF16)
    o_ref[0] = _dot(act, w_ref[0].astype(BF16)) + b_ref[0]


def _adaln(c_pad, w_ada, b_ada):
    depth, d, n = w_ada.shape
    tn = 1536
    return pl.pallas_call(
        _adaln_kernel,
        grid=(depth, n // tn),
        in_specs=[pl.BlockSpec((SUBLANES, d), lambda l, j: (0, 0)),
                  pl.BlockSpec((1, d, tn), lambda l, j: (l, 0, j)),
                  pl.BlockSpec((1, 1, tn), lambda l, j: (l, 0, j))],
        out_specs=pl.BlockSpec((1, SUBLANES, tn), lambda l, j: (l, 0, j)),
        out_shape=jax.ShapeDtypeStruct((depth, SUBLANES, n), F32),
        compiler_params=_cparams(2),
    )(c_pad, w_ada, b_ada.reshape(depth, 1, n))


def _norm_mod(x, g, sc, sh):
    ms = jnp.mean(x * x, axis=-1, keepdims=True)
    h = x * lax.rsqrt(ms + EPS) * g
    return h * (1.0 + sc) + sh


def _nm_matmul_kernel(x_ref, g_ref, sc_ref, sh_ref, w_ref, o_ref, *, chunks):
    h = _norm_mod(x_ref[...], g_ref[...], sc_ref[0], sh_ref[0]).astype(BF16)
    for (c0, c1) in chunks:
        o_ref[:, c0:c1] = _dot(h, w_ref[:, c0:c1]).astype(o_ref.dtype)


def _col_chunks(n, step=1024):
    out, c = [], 0
    while c < n:
        out.append((c, min(c + step, n)))
        c += step
    return tuple(out)


def _nm_matmul(x2, g, sc, sh, w, seq, tm=256):
    m, d = x2.shape
    n = w.shape[1]
    tpb = seq // tm
    return pl.pallas_call(
        functools.partial(_nm_matmul_kernel, chunks=_col_chunks(n)),
        grid=(m // tm,),
        in_specs=[pl.BlockSpec((tm, d), lambda i: (i, 0)),
                  pl.BlockSpec((1, d), lambda i: (0, 0)),
                  pl.BlockSpec((1, 1, d), lambda i: (i // tpb, 0, 0)),
                  pl.BlockSpec((1, 1, d), lambda i: (i // tpb, 0, 0)),
                  pl.BlockSpec((d, n), lambda i: (0, 0))],
        out_specs=pl.BlockSpec((tm, n), lambda i: (i, 0)),
        out_shape=jax.ShapeDtypeStruct((m, n), F32),
        compiler_params=_cparams(1),
    )(x2, g, sc, sh, w)


def _prep_kernel(x_ref, g_ref, sc_ref, sh_ref, w_ref, cos_ref, sin_ref, qg_ref, kg_ref,
                 qo_ref, ko_ref, vo_ref, iqo_ref, iko_ref, iwo_ref):
    tm = x_ref.shape[0]
    half = HEAD_DIM // 2
    cos = cos_ref[...]
    sins = sin_ref[...]
    lane = lax.broadcasted_iota(jnp.int32, (tm, LANES), 1)
    first_half = (lane & (HEAD_DIM - 1)) < half
    r_io = lax.broadcasted_iota(jnp.int32, (LANES, LANES), 0)
    c_io = lax.broadcasted_iota(jnp.int32, (LANES, LANES), 1)
    same_head = jnp.where((r_io < HEAD_DIM) == (c_io < HEAD_DIM), 1.0, 0.0).astype(BF16)

    def rope(x):
        partner = jnp.where(first_half, pltpu.roll(x, LANES - half, 1), pltpu.roll(x, half, 1))
        return x * cos + partner * sins

    def head_norm(x, g):
        sq = x * x
        hi = sq.astype(BF16)
        lo = (sq - hi.astype(F32)).astype(BF16)
        ss = _dot(hi, same_head) + _dot(lo, same_head)
        return x * lax.rsqrt(ss * (1.0 / HEAD_DIM) + EPS) * g

    h = _norm_mod(x_ref[...], g_ref[...], sc_ref[0], sh_ref[0]).astype(BF16)

    def proj(c0, width):
        return _dot(h, w_ref[:, c0:c0 + width])

    qg = qg_ref[...]
    kg = kg_ref[...]
    zeros = jnp.zeros((HEAD_DIM, tm), BF16)
    wide = 2 * LANES
    k_col, v_col, iq_col = ATTN_WIDTH, 2 * ATTN_WIDTH, 3 * ATTN_WIDTH
    misc_col = iq_col + IDX_HEADS * IDX_DIM
    for t2 in range(ATTN_WIDTH // wide):
        qq = proj(t2 * wide, wide)
        kk = proj(k_col + t2 * wide, wide)
        for u in range(2):
            t = 2 * t2 + u
            lanes = slice(u * LANES, (u + 1) * LANES)
            q_t = (rope(head_norm(qq[:, lanes], qg)) * (HEAD_DIM ** -0.5)).T.astype(BF16)
            qo_ref[0, 2 * t, 0:HEAD_DIM, :] = q_t[0:HEAD_DIM]
            qo_ref[0, 2 * t, HEAD_DIM:LANES, :] = zeros
            qo_ref[0, 2 * t + 1, 0:HEAD_DIM, :] = zeros
            qo_ref[0, 2 * t + 1, HEAD_DIM:LANES, :] = q_t[HEAD_DIM:LANES]
            ko_ref[0, t] = rope(head_norm(kk[:, lanes], kg)).astype(BF16)
        vo_ref[0, t2 * wide:(t2 + 1) * wide, :] = proj(v_col + t2 * wide, wide).T.astype(BF16)
    for t2 in range(IDX_HEADS * IDX_DIM // wide):
        pp = proj(iq_col + t2 * wide, wide)
        for u in range(2):
            rows = slice((2 * t2 + u) * LANES, (2 * t2 + u + 1) * LANES)
            iqo_ref[0, rows, :] = rope(pp[:, u * LANES:(u + 1) * LANES]).T.astype(BF16)
    misc = proj(misc_col, LANES)
    iko_ref[0] = rope(misc)[:, IDX_DIM:2 * IDX_DIM].astype(BF16)
    iwo_ref[0] = misc.T[0:IDX_HEADS, :]


def _prep(x2, g, sc, sh, w, cos64, sin64, qg, kg, bsz, seq, tm=256):
    tpb = seq // tm
    row = lambda b, i: b * tpb + i
    d = x2.shape[1]
    aw = ATTN_WIDTH
    iqw = IDX_HEADS * IDX_DIM
    return pl.pallas_call(
        _prep_kernel,
        grid=(bsz, tpb),
        in_specs=[pl.BlockSpec((tm, d), lambda b, i: (row(b, i), 0)),
                  pl.BlockSpec((1, d), lambda b, i: (0, 0)),
                  pl.BlockSpec((1, 1, d), lambda b, i: (b, 0, 0)),
                  pl.BlockSpec((1, 1, d), lambda b, i: (b, 0, 0)),
                  pl.BlockSpec((d, A_WIDTH), lambda b, i: (0, 0)),
                  pl.BlockSpec((tm, LANES), lambda b, i: (row(b, i), 0)),
                  pl.BlockSpec((tm, LANES), lambda b, i: (row(b, i), 0)),
                  pl.BlockSpec((1, LANES), lambda b, i: (0, 0)),
                  pl.BlockSpec((1, LANES), lambda b, i: (0, 0))],
        out_specs=[pl.BlockSpec((1, ATTN_HEADS, LANES, tm), lambda b, i: (b, 0, 0, i)),
                   pl.BlockSpec((1, aw // LANES, tm, LANES), lambda b, i: (b, 0, i, 0)),
                   pl.BlockSpec((1, aw, tm), lambda b, i: (b, 0, i)),
                   pl.BlockSpec((1, iqw, tm), lambda b, i: (b, 0, i)),
                   pl.BlockSpec((1, tm, IDX_DIM), lambda b, i: (b, i, 0)),
                   pl.BlockSpec((1, IDX_HEADS, tm), lambda b, i: (b, 0, i))],
        out_shape=[jax.ShapeDtypeStruct((bsz, ATTN_HEADS, LANES, seq), BF16),
                   jax.ShapeDtypeStruct((bsz, aw // LANES, seq, LANES), BF16),
                   jax.ShapeDtypeStruct((bsz, aw, seq), BF16),
                   jax.ShapeDtypeStruct((bsz, iqw, seq), BF16),
                   jax.ShapeDtypeStruct((bsz, seq, IDX_DIM), BF16),
                   jax.ShapeDtypeStruct((bsz, IDX_HEADS, seq), F32)],
        compiler_params=_cparams(2),
    )(x2, g, sc, sh, w, cos64, sin64, qg, kg)


def _select_kernel(iq_ref, ik_ref, iw_ref, mask_ref, key_ref, *, tq, ck, n_keep, seq):
    i = pl.program_id(1)
    n_ch = ((i + 1) * tq + ck - 1) // ck
    w = iw_ref[0]
    q_pos = i * tq + lax.broadcasted_iota(jnp.int32, (ck, tq), 1)
    k_iota = lax.broadcasted_iota(jnp.int32, (ck, tq), 0)

    def score_chunk(c, carry):
        off = pl.multiple_of(c * ck, ck)
        kt = ik_ref[0, pl.ds(off, ck), :]
        acc = jnp.zeros((ck, tq), F32)
        for h in range(IDX_HEADS):
            logit = _dot(kt, iq_ref[0, h * IDX_DIM:(h + 1) * IDX_DIM, :])
            acc = acc + w[h:h + 1, :] * jnp.maximum(logit, 0.0)
        score = acc * IDX_SCALE
        key = _float_bits_to_key(pltpu.bitcast(score, jnp.int32))
        key_ref[pl.ds(off, ck), :] = jnp.where(k_iota + off <= q_pos, key, KEY_NEG_INF)
        return carry

    lax.fori_loop(0, n_ch, score_chunk, 0)

    cnt_rows = 4 * SUBLANES

    def count_ge(cand):
        def body(c, acc):
            off = pl.multiple_of(c * ck, ck)
            hit = jnp.where(key_ref[pl.ds(off, ck), :] >= cand, 1.0, 0.0)
            return acc + jnp.sum(hit.reshape(ck // cnt_rows, cnt_rows, tq), axis=0)
        acc = lax.fori_loop(0, n_ch, body, jnp.zeros((cnt_rows, tq), F32))
        return jnp.sum(acc, axis=0, keepdims=True)

    kf = jnp.float32(n_keep)

    def bit_step(it, state):
        t_u, c_ge, c_gt = state
        bit = lax.shift_left(jnp.int32(1), 31 - it)
        cand_u = t_u | bit
        cnt = count_ge(cand_u ^ INT_MIN)
        keep = cnt >= kf
        return jnp.where(keep, cand_u, t_u), jnp.where(keep, cnt, c_ge), jnp.where(keep, c_gt, cnt)

    n_all = (n_ch * ck).astype(F32) + jnp.zeros((1, tq), F32)
    t_u, cnt_ge, cnt_gt = lax.fori_loop(0, 32, bit_step, (jnp.zeros((1, tq), jnp.int32), n_all,
                                                          jnp.zeros((1, tq), F32)))
    thr = t_u ^ INT_MIN
    need = kf - cnt_gt
    tie = jnp.where((cnt_ge > kf) & (thr > KEY_NEG_INF), 1, 0)
    any_tie = jnp.max(tie) > 0
    thr_eff = jnp.maximum(thr, KEY_NEG_INF + 1)

    @pl.when(jnp.logical_not(any_tie))
    def _():
        def body(c, carry):
            off = pl.multiple_of(c * ck, ck)
            kc = key_ref[pl.ds(off, ck), :]
            mask_ref[0, pl.ds(off, ck), :] = jnp.where(kc >= thr_eff, 1, 0).astype(jnp.int8)
            return carry
        lax.fori_loop(0, n_ch, body, 0)

    @pl.when(any_tie)
    def _():
        lower = jnp.where(lax.broadcasted_iota(jnp.int32, (ck, ck), 1)
                          < lax.broadcasted_iota(jnp.int32, (ck, ck), 0), 1.0, 0.0).astype(BF16)

        def body(c, seen):
            off = pl.multiple_of(c * ck, ck)
            kc = key_ref[pl.ds(off, ck), :]
            eq = jnp.where(kc == thr, 1.0, 0.0)
            rank = seen + _dot(lower, eq.astype(BF16))
            take_eq = jnp.where(rank < need, eq, 0.0)
            sel = jnp.where(kc > thr, 1.0, take_eq)
            sel = jnp.where(kc > KEY_NEG_INF, sel, 0.0)
            mask_ref[0, pl.ds(off, ck), :] = sel.astype(jnp.int32).astype(jnp.int8)
            return seen + jnp.sum(eq, axis=0, keepdims=True)
        lax.fori_loop(0, n_ch, body, jnp.zeros((1, tq), F32))

    def zero_tail(c, carry):
        off = pl.multiple_of(c * ck, ck)
        mask_ref[0, pl.ds(off, ck), :] = jnp.zeros((ck, tq), jnp.int8)
        return carry
    lax.fori_loop(n_ch, seq // ck, zero_tail, 0)


def _select(iq_t, ik, iw_t, bsz, seq, n_keep, tq=256, ck=512):
    ck = min(ck, seq)
    return pl.pallas_call(
        functools.partial(_select_kernel, tq=tq, ck=ck, n_keep=n_keep, seq=seq),
        grid=(bsz, seq // tq),
        in_specs=[pl.BlockSpec((1, IDX_HEADS * IDX_DIM, tq), lambda b, i: (b, 0, i)),
                  pl.BlockSpec((1, seq, IDX_DIM), lambda b, i: (b, 0, 0)),
                  pl.BlockSpec((1, IDX_HEADS, tq), lambda b, i: (b, 0, i))],
        out_specs=pl.BlockSpec((1, seq, tq), lambda b, i: (b, 0, i)),
        out_shape=jax.ShapeDtypeStruct((bsz, seq, seq), jnp.int8),
        scratch_shapes=[pltpu.VMEM((seq, tq), jnp.int32)],
        compiler_params=_cparams(2),
    )(iq_t, ik, iw_t)


def _attn_kernel(qi_ref, kj_ref, q_ref, k_ref, v_ref, mask_ref, o_ref, m_ref, l_ref, acc_ref, bias_ref,
                 s0_ref, s1_ref, *, tq, tk):
    t = pl.program_id(1)
    i = qi_ref[t]
    j = kj_ref[t]
    last_j = ((i + 1) * tq - 1) // tk

    @pl.when(j == 0)
    def _():
        m_ref[...] = jnp.full(m_ref.shape, NEG_BIG, F32)
        l_ref[...] = jnp.zeros(l_ref.shape, F32)
        acc_ref[...] = jnp.zeros(acc_ref.shape, F32)

    bias_ref[...] = jnp.where(mask_ref[0].astype(jnp.int32) != 0, 0.0, NEG_BIG)

    def scores(h, s_ref):
        qh = q_ref[0, h]
        for part in range(S_PARTS):
            ks = slice(part * (tk // S_PARTS), (part + 1) * (tk // S_PARTS))
            s_ref[ks, :] = _dot(k_ref[0, h // 2, ks, :], qh) + bias_ref[ks, :]

    def softmax_pv(h, s_ref):
        rows = slice(h * HEAD_DIM, (h + 1) * HEAD_DIM)
        s = s_ref[...]
        m_old = m_ref[h:h + 1, :]
        m_new = jnp.maximum(m_old, jnp.max(s, axis=0, keepdims=True))
        alpha = jnp.exp(m_old - m_new)
        p = jnp.exp(s - m_new)
        l_ref[h:h + 1, :] = alpha * l_ref[h:h + 1, :] + jnp.sum(p, axis=0, keepdims=True)
        acc_ref[rows, :] = alpha * acc_ref[rows, :] + _dot(v_ref[0, rows, :], p.astype(BF16))
        m_ref[h:h + 1, :] = m_new

    bufs = (s0_ref, s1_ref)
    scores(0, bufs[0])
    for h in range(ATTN_HEADS):
        if h + 1 < ATTN_HEADS:
            scores(h + 1, bufs[(h + 1) % 2])
        softmax_pv(h, bufs[h % 2])

    @pl.when(j == last_j)
    def _():
        for h in range(ATTN_HEADS):
            rows = slice(h * HEAD_DIM, (h + 1) * HEAD_DIM)
            acc_ref[rows, :] = acc_ref[rows, :] / l_ref[h:h + 1, :]
        o_ref[0] = acc_ref[...].T.astype(o_ref.dtype)


def _attention(q_t, k_hm, v_t, mask_t, bsz, seq, tq=256, tk=512):
    tk = min(tk, seq)
    tq = min(tq, tk)
    pairs = [(i, j) for i in range(seq // tq) for j in range(((i + 1) * tq - 1) // tk + 1)]
    qi = jnp.asarray([p[0] for p in pairs], jnp.int32)
    kj = jnp.asarray([p[1] for p in pairs], jnp.int32)
    grid_spec = pltpu.PrefetchScalarGridSpec(
        num_scalar_prefetch=2,
        grid=(bsz, len(pairs)),
        in_specs=[pl.BlockSpec((1, ATTN_HEADS, LANES, tq), lambda b, t, qi, kj: (b, 0, 0, qi[t])),
                  pl.BlockSpec((1, ATTN_WIDTH // LANES, tk, LANES), lambda b, t, qi, kj: (b, 0, kj[t], 0)),
                  pl.BlockSpec((1, ATTN_WIDTH, tk), lambda b, t, qi, kj: (b, 0, kj[t])),
                  pl.BlockSpec((1, tk, tq), lambda b, t, qi, kj: (b, kj[t], qi[t]))],
        out_specs=pl.BlockSpec((1, tq, ATTN_WIDTH), lambda b, t, qi, kj: (b, qi[t], 0)),
        scratch_shapes=[pltpu.VMEM((ATTN_HEADS, tq), F32),
                        pltpu.VMEM((ATTN_HEADS, tq), F32),
                        pltpu.VMEM((ATTN_WIDTH, tq), F32),
                        pltpu.VMEM((tk, tq), F32),
                        pltpu.VMEM((tk, tq), F32),
                        pltpu.VMEM((tk, tq), F32)])
    return pl.pallas_call(
        functools.partial(_attn_kernel, tq=tq, tk=tk),
        grid_spec=grid_spec,
        out_shape=jax.ShapeDtypeStruct((bsz, seq, ATTN_WIDTH), BF16),
        compiler_params=_cparams(2),
    )(qi, kj, q_t, k_hm, v_t, mask_t)


def _ssd_kernel(z_ref, xs_ref, bc_ref, dt_ref, cwx_ref, cbx_ref, cwb_ref, cbb_ref, dtb_ref, alog_ref,
                dskip_ref, gain_ref, expand_ref, o_ref, extx_ref, extb_ref, state_ref, y_ref, *, t):
    c = pl.program_id(1)
    halo = SUBLANES

    @pl.when(c == 0)
    def _():
        extx_ref[0:halo, :] = jnp.zeros((halo, D_INNER), F32)
        extb_ref[0:halo, :] = jnp.zeros((halo, BC_WIDTH), F32)
        state_ref[...] = jnp.zeros(state_ref.shape, F32)

    def conv_silu(ext_ref, raw_ref, w_ref, b_ref):
        ext_ref[halo:halo + t, :] = raw_ref[...]
        y = b_ref[...] + w_ref[SSM_CONV - 1:SSM_CONV, :] * ext_ref[halo:halo + t, :]
        for k in range(SSM_CONV - 1):
            shift = SSM_CONV - 1 - k
            y = y + w_ref[k:k + 1, :] * ext_ref[halo - shift:halo - shift + t, :]
        ext_ref[0:halo, :] = ext_ref[t:t + halo, :]
        return _silu(y)

    xs = conv_silu(extx_ref, xs_ref, cwx_ref, cbx_ref)
    bc = conv_silu(extb_ref, bc_ref, cwb_ref, cbb_ref)

    dtv = dt_ref[...] + dtb_ref[...]
    dt = jnp.maximum(dtv, 0.0) + jnp.log1p(jnp.exp(-jnp.abs(dtv)))
    a = dt * (-jnp.exp(alog_ref[...]))

    r_io = lax.broadcasted_iota(jnp.int32, (t, t), 0)
    c_io = lax.broadcasted_iota(jnp.int32, (t, t), 1)
    tri = r_io >= c_io
    upper_incl = jnp.where(r_io <= c_io, 1.0, 0.0).astype(BF16)
    lower_incl = jnp.where(tri, 1.0, 0.0).astype(BF16)
    a_cs = _dot3_left(lower_incl, a)
    a_cs_t = _dot3(a.T, upper_incl)
    expand = expand_ref[...]
    dt_x = _dot3(dt, expand)
    acs_x = _dot3(a_cs, expand)
    alast_x = acs_x[t - 1:t, :]
    xdt = xs * dt_x
    e_start = jnp.exp(acs_x)
    xw = (xdt * jnp.exp(alast_x - acs_x)).astype(BF16)
    xdt_b = xdt.astype(BF16)
    chunk_decay = jnp.exp(alast_x)
    left_half = lax.broadcasted_iota(jnp.int32, (t, LANES), 1) < SSM_HEAD_DIM

    gw = SSM_GROUPS * D_STATE
    for g in range(SSM_GROUPS):
        bm = bc[:, g * D_STATE:(g + 1) * D_STATE]
        cm = bc[:, gw + g * D_STATE:gw + (g + 1) * D_STATE].astype(BF16)
        bm_t = bm.T.astype(BF16)
        cb = _dot_nt(cm, bm.astype(BF16))
        for p in range(HEADS_PER_GROUP // 2):
            pair = g * (HEADS_PER_GROUP // 2) + p
            lanes = slice(pair * LANES, (pair + 1) * LANES)
            xp = xdt_b[:, lanes]
            ys = []
            for e in range(2):
                h = 2 * pair + e
                seg = a_cs[:, h:h + 1] - a_cs_t[h:h + 1, :]
                dec = jnp.exp(jnp.where(tri, seg, -jnp.inf))
                ys.append(_dot((cb * dec).astype(BF16), xp))
            y_diag = jnp.where(left_half, ys[0], ys[1])
            st = state_ref[pair]
            y_off = _dot(cm, st.astype(BF16)) * e_start[:, lanes]
            y_ref[:, lanes] = y_diag + y_off
            state_ref[pair] = st * chunk_decay[:, lanes] + _dot(bm_t, xw[:, lanes])

    y = (y_ref[...] + xs * dskip_ref[...]) * _silu(z_ref[...])
    gwid = D_INNER // SSM_GROUPS
    for g in range(SSM_GROUPS):
        yg = y[:, g * gwid:(g + 1) * gwid]
        ms = jnp.mean(yg * yg, axis=-1, keepdims=True)
        o_ref[0, :, g * gwid:(g + 1) * gwid] = (yg * lax.rsqrt(ms + EPS)
                                                 * gain_ref[:, g * gwid:(g + 1) * gwid]).astype(o_ref.dtype)


def _ssd(b_proj, cwx, cbx, cwb, cbb, dtb, alog, dskip_x, gain, expand, bsz, seq):
    t = SSM_CHUNK
    nc = seq // t
    row = lambda b, c: b * nc + c
    const = lambda shape: pl.BlockSpec(shape, lambda b, c: (0,) * len(shape))
    return pl.pallas_call(
        functools.partial(_ssd_kernel, t=t),
        grid=(bsz, nc),
        in_specs=[pl.BlockSpec((t, D_INNER), lambda b, c: (row(b, c), 0)),
                  pl.BlockSpec((t, D_INNER), lambda b, c: (row(b, c), 1)),
                  pl.BlockSpec((t, BC_WIDTH), lambda b, c: (row(b, c), 2 * D_INNER // BC_WIDTH)),
                  pl.BlockSpec((t, LANES), lambda b, c: (row(b, c), (2 * D_INNER + BC_WIDTH) // LANES)),
                  const((SSM_CONV, D_INNER)), const((1, D_INNER)),
                  const((SSM_CONV, BC_WIDTH)), const((1, BC_WIDTH)),
                  const((1, LANES)), const((1, LANES)),
                  const((1, D_INNER)), const((1, D_INNER)),
                  const((LANES, D_INNER))],
        out_specs=pl.BlockSpec((1, t, D_INNER), lambda b, c: (b, c, 0)),
        out_shape=jax.ShapeDtypeStruct((bsz, seq, D_INNER), BF16),
        scratch_shapes=[pltpu.VMEM((t + SUBLANES, D_INNER), F32),
                        pltpu.VMEM((t + SUBLANES, BC_WIDTH), F32),
                        pltpu.VMEM((SSM_HEADS // 2, D_STATE, LANES), F32),
                        pltpu.VMEM((t, D_INNER), F32)],
        compiler_params=_cparams(2),
    )(b_proj, b_proj, b_proj, b_proj, cwx, cbx, cwb, cbb, dtb, alog, dskip_x, gain, expand)


def _merge_kernel(x_ref, oa_ref, ys_ref, ga_ref, gm_ref, g1_ref, wa_ref, ws_ref, wo_ref, o_ref):
    o_attn = _dot(oa_ref[...], wa_ref[...])
    o_ssm = _dot(ys_ref[...], ws_ref[...])
    mix = _sigmoid(ga_ref[...]) * o_attn + _sigmoid(gm_ref[...]) * o_ssm
    o_ref[...] = x_ref[...] + g1_ref[0] * _dot(mix.astype(BF16), wo_ref[...])


def _merge(x2, oa2, ys2, c_proj, g1, wa, ws, wo, seq, tm=256):
    m, d = x2.shape
    tpb = seq // tm
    const = lambda shape: pl.BlockSpec(shape, lambda i: (0,) * len(shape))
    return pl.pallas_call(
        _merge_kernel,
        grid=(m // tm,),
        in_specs=[pl.BlockSpec((tm, d), lambda i: (i, 0)),
                  pl.BlockSpec((tm, ATTN_WIDTH), lambda i: (i, 0)),
                  pl.BlockSpec((tm, D_INNER), lambda i: (i, 0)),
                  pl.BlockSpec((tm, d), lambda i: (i, 0)),
                  pl.BlockSpec((tm, d), lambda i: (i, 1)),
                  pl.BlockSpec((1, 1, d), lambda i: (i // tpb, 0, 0)),
                  const((ATTN_WIDTH, d)), const((D_INNER, d)), const((d, d))],
        out_specs=pl.BlockSpec((tm, d), lambda i: (i, 0)),
        out_shape=jax.ShapeDtypeStruct((m, d), F32),
        compiler_params=_cparams(1),
    )(x2, oa2, ys2, c_proj, c_proj, g1, wa, ws, wo)


def _ffn_kernel(x_ref, g_ref, sc_ref, sh_ref, g2_ref, wup_ref, cw_ref, cb_ref, wdn_ref, o_ref, ext_ref, *, tm):
    i = pl.program_id(1)
    halo = SUBLANES

    @pl.when(i == 0)
    def _():
        ext_ref[0:halo, :] = jnp.zeros((halo, 2 * D_FF), F32)

    x = x_ref[...]
    h = _norm_mod(x, g_ref[...], sc_ref[0], sh_ref[0]).astype(BF16)
    for (c0, c1) in _col_chunks(2 * D_FF, 1408):
        ext_ref[halo:halo + tm, c0:c1] = _dot(h, wup_ref[:, c0:c1])

    def conv(c0, c1):
        u = cb_ref[:, c0:c1] + cw_ref[FFN_CONV - 1:FFN_CONV, c0:c1] * ext_ref[halo:halo + tm, c0:c1]
        for k in range(FFN_CONV - 1):
            shift = FFN_CONV - 1 - k
            u = u + cw_ref[k:k + 1, c0:c1] * ext_ref[halo - shift:halo - shift + tm, c0:c1]
        return u

    out = jnp.zeros((tm, D_MODEL), F32)
    for (c0, c1) in _col_chunks(D_FF, 1408):
        act = _silu(conv(D_FF + c0, D_FF + c1)) * conv(c0, c1)
        out = out + _dot(act.astype(BF16), wdn_ref[c0:c1, :])
    o_ref[...] = x + g2_ref[0] * out
    ext_ref[0:halo, :] = ext_ref[tm:tm + halo, :]


def _ffn(x2, g, sc, sh, g2, wup, cw, cb, wdn, bsz, seq, tm=256):
    m, d = x2.shape
    tpb = seq // tm
    row = lambda b, i: b * tpb + i
    const = lambda shape: pl.BlockSpec(shape, lambda b, i: (0,) * len(shape))
    return pl.pallas_call(
        functools.partial(_ffn_kernel, tm=tm),
        grid=(bsz, tpb),
        in_specs=[pl.BlockSpec((tm, d), lambda b, i: (row(b, i), 0)),
                  const((1, d)),
                  pl.BlockSpec((1, 1, d), lambda b, i: (b, 0, 0)),
                  pl.BlockSpec((1, 1, d), lambda b, i: (b, 0, 0)),
                  pl.BlockSpec((1, 1, d), lambda b, i: (b, 0, 0)),
                  const((d, 2 * D_FF)), const((FFN_CONV, 2 * D_FF)), const((1, 2 * D_FF)),
                  const((D_FF, d))],
        out_specs=pl.BlockSpec((tm, d), lambda b, i: (row(b, i), 0)),
        out_shape=jax.ShapeDtypeStruct((m, d), F32),
        scratch_shapes=[pltpu.VMEM((tm + SUBLANES, 2 * D_FF), F32)],
        compiler_params=_cparams(2),
    )(x2, g, sc, sh, g2, wup, cw, cb, wdn)


def kernel(x, c, positions, w_ada, b_ada, norm1_g, w_in, q_norm_g, k_norm_g, ssm_conv_w, ssm_conv_b, dt_bias,
           a_log, d_skip, ssm_norm_g, w_attn_o, w_ssm_o, w_out, norm2_g, w_up, ffn_conv_w, ffn_conv_b, w_down):
    bsz, seq, d = x.shape
    depth = w_ada.shape[0]
    n_keep = min(TOPK_MAX, seq // 4)
    m = bsz * seq

    inv = 1.0 / (ROPE_THETA ** (jnp.arange(0, HEAD_DIM, 2, dtype=F32) / HEAD_DIM))
    ang = positions.astype(F32)[..., None] * inv
    cos64 = jnp.concatenate([jnp.cos(ang)] * 4, axis=-1).reshape(m, LANES)
    sin64 = jnp.concatenate([-jnp.sin(ang), jnp.sin(ang)] * 2, axis=-1).reshape(m, LANES)

    sp = np.cumsum((0, ATTN_WIDTH, ATTN_WIDTH, ATTN_WIDTH, IDX_HEADS * IDX_DIM, IDX_DIM, IDX_HEADS, D_INNER,
                    D_INNER + BC_WIDTH, SSM_HEADS, D_MODEL, D_MODEL))
    col = lambda k: w_in[:, :, sp[k]:sp[k + 1]]
    zpad = lambda n: jnp.zeros((depth, d, n), w_in.dtype)
    w_a = jnp.concatenate([col(0), col(1), col(2), col(3), col(5), zpad(LANES - IDX_DIM - IDX_HEADS), col(4)],
                          axis=-1).astype(BF16)
    w_b = jnp.concatenate([col(6), col(7), col(8), zpad(LANES - SSM_HEADS)], axis=-1).astype(BF16)
    w_c = jnp.concatenate([col(9), col(10)], axis=-1).astype(BF16)
    wa_o, ws_o, wo = w_attn_o.astype(BF16), w_ssm_o.astype(BF16), w_out.astype(BF16)
    wup, wdn = w_up.astype(BF16), w_down.astype(BF16)

    lane_pad = lambda v: jnp.pad(v, ((0, 0), (0, LANES - v.shape[-1])))
    dtb = lane_pad(dt_bias.astype(F32))
    alog = lane_pad(a_log.astype(F32))
    dskip_x = jnp.repeat(d_skip.astype(F32), SSM_HEAD_DIM, axis=-1)
    expand = (jnp.arange(LANES)[:, None] == (jnp.arange(D_INNER)[None, :] // SSM_HEAD_DIM)).astype(BF16)

    c_pad = jnp.pad(c.astype(F32), ((0, SUBLANES - bsz), (0, 0)))
    mod = _adaln(c_pad, w_ada, b_ada)[:, :bsz]

    x2 = x.reshape(m, d)
    for l in range(depth):
        sh1, sc1, g1, sh2, sc2, g2 = [mod[l, :, k * d:(k + 1) * d].reshape(bsz, 1, d) for k in range(6)]
        g_n1 = norm1_g[l].reshape(1, d)
        b_proj = _nm_matmul(x2, g_n1, sc1, sh1, w_b[l], seq)
        c_proj = _nm_matmul(x2, g_n1, sc1, sh1, w_c[l], seq)

        q_t, k_hm, v_t, iq_t, ik, iw_t = _prep(x2, g_n1, sc1, sh1, w_a[l], cos64, sin64,
                                               jnp.tile(q_norm_g[l], 2).reshape(1, LANES),
                                               jnp.tile(k_norm_g[l], 2).reshape(1, LANES), bsz, seq)
        mask_t = _select(iq_t, ik, iw_t, bsz, seq, n_keep)
        o_attn = _attention(q_t, k_hm, v_t, mask_t, bsz, seq)

        y_ssm = _ssd(b_proj, ssm_conv_w[l, :, :D_INNER], ssm_conv_b[l, :D_INNER].reshape(1, D_INNER),
                     ssm_conv_w[l, :, D_INNER:], ssm_conv_b[l, D_INNER:].reshape(1, BC_WIDTH),
                     dtb[l].reshape(1, LANES), alog[l].reshape(1, LANES), dskip_x[l].reshape(1, D_INNER),
                     ssm_norm_g[l].reshape(1, D_INNER), expand, bsz, seq)

        x2 = _merge(x2, o_attn.reshape(m, ATTN_WIDTH), y_ssm.reshape(m, D_INNER), c_proj, g1,
                    wa_o[l], ws_o[l], wo[l], seq)
        x2 = _ffn(x2, norm2_g[l].reshape(1, d), sc2, sh2, g2, wup[l], ffn_conv_w[l],
                  ffn_conv_b[l].reshape(1, 2 * D_FF), wdn[l], bsz, seq)
    return x2.reshape(bsz, seq, d)
```

```python
import functools
import math

import numpy as np
import jax
import jax.numpy as jnp
from jax import lax
from jax.experimental import pallas as pl
from jax.experimental.pallas import tpu as pltpu

F32 = jnp.float32
BF16 = jnp.bfloat16

D_MODEL = 1024
ATTN_HEADS = 16
HEAD_DIM = 64
ATTN_WIDTH = ATTN_HEADS * HEAD_DIM
IDX_HEADS = 8
IDX_DIM = 64
IDX_SCALE = (IDX_HEADS ** -0.5) * (IDX_DIM ** -0.5)
TOPK_MAX = 256
ROPE_THETA = 10000.0
D_INNER = 2 * D_MODEL
SSM_HEAD_DIM = 64
SSM_HEADS = D_INNER // SSM_HEAD_DIM
SSM_GROUPS = 4
HEADS_PER_GROUP = SSM_HEADS // SSM_GROUPS
D_STATE = 128
SSM_CONV = 4
SSM_CHUNK = 256
BC_WIDTH = 2 * SSM_GROUPS * D_STATE
D_FF = 2816
FFN_CONV = 3
EPS = 1e-6

LANES = 128
SUBLANES = 8
VMEM_LIMIT = 56 * 1024 * 1024

A_WIDTH = 3 * ATTN_WIDTH + IDX_HEADS * IDX_DIM + LANES
B_WIDTH = 2 * D_INNER + BC_WIDTH + LANES
C_WIDTH = 2 * D_MODEL

NEG_BIG = -1e30
Q_SCALE = HEAD_DIM ** -0.5 * math.log2(math.e)
V_ROWS = HEAD_DIM + 16
INT_MIN = -2 ** 31
KEY_NEG_INF = -0x7F800000
S_PARTS = 4


def _float_bits_to_key(bits):
    return jnp.where(bits >= 0, bits, -(bits & 0x7FFFFFFF))


def _cparams(n_axes):
    return pltpu.CompilerParams(dimension_semantics=("arbitrary",) * n_axes,
                                vmem_limit_bytes=VMEM_LIMIT)


def _sigmoid(x):
    return 1.0 / (1.0 + jnp.exp(-x))


def _silu(x):
    return x * _sigmoid(x)


def _dot(a, b):
    return jnp.dot(a, b, preferred_element_type=F32)


def _dot_nt(a, b):
    return lax.dot_general(a, b, (((1,), (1,)), ((), ())), preferred_element_type=F32)


def _split3(a):
    hi = a.astype(BF16)
    r1 = a - hi.astype(F32)
    mid = r1.astype(BF16)
    lo = (r1 - mid.astype(F32)).astype(BF16)
    return hi, mid, lo


def _dot3(a, b01):
    hi, mid, lo = _split3(a)
    return _dot(hi, b01) + _dot(mid, b01) + _dot(lo, b01)


def _dot3_left(b01, a):
    hi, mid, lo = _split3(a)
    return _dot(b01, hi) + _dot(b01, mid) + _dot(b01, lo)


def _adaln_kernel(c_ref, w_ref, b_ref, o_ref):
    c = c_ref[...]
    act = _silu(c).astype(BF16)
    o_ref[0] = _dot(act, w_ref[0].astype(BF16)) + b_ref[0]


def _adaln(c_pad, w_ada, b_ada):
    depth, d, n = w_ada.shape
    tn = 1536
    return pl.pallas_call(
        _adaln_kernel,
        grid=(depth, n // tn),
        in_specs=[pl.BlockSpec((SUBLANES, d), lambda l, j: (0, 0)),
                  pl.BlockSpec((1, d, tn), lambda l, j: (l, 0, j)),
                  pl.BlockSpec((1, 1, tn), lambda l, j: (l, 0, j))],
        out_specs=pl.BlockSpec((1, SUBLANES, tn), lambda l, j: (l, 0, j)),
        out_shape=jax.ShapeDtypeStruct((depth, SUBLANES, n), F32),
        compiler_params=_cparams(2),
    )(c_pad, w_ada, b_ada.reshape(depth, 1, n))


def _norm_mod(x, g, sc, sh):
    ms = jnp.mean(x * x, axis=-1, keepdims=True)
    h = x * lax.rsqrt(ms + EPS) * g
    return h * (1.0 + sc) + sh


def _nm_matmul_kernel(x_ref, g_ref, sc_ref, sh_ref, w_ref, o_ref, *, chunks):
    h = _norm_mod(x_ref[...], g_ref[...], sc_ref[0], sh_ref[0]).astype(BF16)
    for (c0, c1) in chunks:
        o_ref[:, c0:c1] = _dot(h, w_ref[:, c0:c1]).astype(o_ref.dtype)


def _col_chunks(n, step=1024):
    out, c = [], 0
    while c < n:
        out.append((c, min(c + step, n)))
        c += step
    return tuple(out)


def _nm_matmul(x2, g, sc, sh, w, seq, tm=256):
    m, d = x2.shape
    n = w.shape[1]
    tpb = seq // tm
    return pl.pallas_call(
        functools.partial(_nm_matmul_kernel, chunks=_col_chunks(n)),
        grid=(m // tm,),
        in_specs=[pl.BlockSpec((tm, d), lambda i: (i, 0)),
                  pl.BlockSpec((1, d), lambda i: (0, 0)),
                  pl.BlockSpec((1, 1, d), lambda i: (i // tpb, 0, 0)),
                  pl.BlockSpec((1, 1, d), lambda i: (i // tpb, 0, 0)),
                  pl.BlockSpec((d, n), lambda i: (0, 0))],
        out_specs=pl.BlockSpec((tm, n), lambda i: (i, 0)),
        out_shape=jax.ShapeDtypeStruct((m, n), F32),
        compiler_params=_cparams(1),
    )(x2, g, sc, sh, w)


def _prep_kernel(x_ref, g_ref, sc_ref, sh_ref, w_ref, cos_ref, sin_ref, qg_ref, kg_ref,
                 qo_ref, ko_ref, vo_ref, iqo_ref, iko_ref, iwo_ref):
    tm = x_ref.shape[0]
    half = HEAD_DIM // 2
    cos = cos_ref[...]
    sins = sin_ref[...]
    lane = lax.broadcasted_iota(jnp.int32, (tm, LANES), 1)
    first_half = (lane & (HEAD_DIM - 1)) < half
    r_io = lax.broadcasted_iota(jnp.int32, (LANES, LANES), 0)
    c_io = lax.broadcasted_iota(jnp.int32, (LANES, LANES), 1)
    same_head = jnp.where((r_io < HEAD_DIM) == (c_io < HEAD_DIM), 1.0, 0.0).astype(BF16)

    def rope(x):
        partner = jnp.where(first_half, pltpu.roll(x, LANES - half, 1), pltpu.roll(x, half, 1))
        return x * cos + partner * sins

    def head_norm(x, g):
        sq = x * x
        hi = sq.astype(BF16)
        lo = (sq - hi.astype(F32)).astype(BF16)
        ss = _dot(hi, same_head) + _dot(lo, same_head)
        return x * lax.rsqrt(ss * (1.0 / HEAD_DIM) + EPS) * g

    h = _norm_mod(x_ref[...], g_ref[...], sc_ref[0], sh_ref[0]).astype(BF16)

    def proj(c0, width):
        return _dot(h, w_ref[:, c0:c0 + width])

    qg = qg_ref[...]
    kg = kg_ref[...]
    zeros = jnp.zeros((HEAD_DIM, tm), BF16)
    ones_row = jnp.where(lax.broadcasted_iota(jnp.int32, (V_ROWS - HEAD_DIM, tm), 0) == 0, 1.0, 0.0).astype(BF16)
    wide = 2 * LANES
    k_col, v_col, iq_col = ATTN_WIDTH, 2 * ATTN_WIDTH, 3 * ATTN_WIDTH
    misc_col = iq_col + IDX_HEADS * IDX_DIM
    for t2 in range(ATTN_WIDTH // wide):
        qq = proj(t2 * wide, wide)
        kk = proj(k_col + t2 * wide, wide)
        for u in range(2):
            t = 2 * t2 + u
            lanes = slice(u * LANES, (u + 1) * LANES)
            q_t = (rope(head_norm(qq[:, lanes], qg)) * Q_SCALE).T.astype(BF16)
            qo_ref[0, 2 * t, 0:HEAD_DIM, :] = q_t[0:HEAD_DIM]
            qo_ref[0, 2 * t, HEAD_DIM:LANES, :] = zeros
            qo_ref[0, 2 * t + 1, 0:HEAD_DIM, :] = zeros
            qo_ref[0, 2 * t + 1, HEAD_DIM:LANES, :] = q_t[HEAD_DIM:LANES]
            ko_ref[0, t] = rope(head_norm(kk[:, lanes], kg)).astype(BF16)
        v_t = proj(v_col + t2 * wide, wide).T.astype(BF16)
        for u in range(wide // HEAD_DIM):
            hv = t2 * (wide // HEAD_DIM) + u
            vo_ref[0, hv, 0:HEAD_DIM, :] = v_t[u * HEAD_DIM:(u + 1) * HEAD_DIM]
            vo_ref[0, hv, HEAD_DIM:V_ROWS, :] = ones_row
    for t2 in range(IDX_HEADS * IDX_DIM // wide):
        pp = proj(iq_col + t2 * wide, wide)
        for u in range(2):
            rows = slice((2 * t2 + u) * LANES, (2 * t2 + u + 1) * LANES)
            iqo_ref[0, rows, :] = rope(pp[:, u * LANES:(u + 1) * LANES]).T.astype(BF16)
    misc = proj(misc_col, LANES)
    iko_ref[0] = rope(misc)[:, IDX_DIM:2 * IDX_DIM].astype(BF16)
    iwo_ref[0] = misc.T[0:IDX_HEADS, :]


def _prep(x2, g, sc, sh, w, cos64, sin64, qg, kg, bsz, seq, tm=256):
    tpb = seq // tm
    row = lambda b, i: b * tpb + i
    d = x2.shape[1]
    aw = ATTN_WIDTH
    iqw = IDX_HEADS * IDX_DIM
    return pl.pallas_call(
        _prep_kernel,
        grid=(bsz, tpb),
        in_specs=[pl.BlockSpec((tm, d), lambda b, i: (row(b, i), 0)),
                  pl.BlockSpec((1, d), lambda b, i: (0, 0)),
                  pl.BlockSpec((1, 1, d), lambda b, i: (b, 0, 0)),
                  pl.BlockSpec((1, 1, d), lambda b, i: (b, 0, 0)),
                  pl.BlockSpec((d, A_WIDTH), lambda b, i: (0, 0)),
                  pl.BlockSpec((tm, LANES), lambda b, i: (row(b, i), 0)),
                  pl.BlockSpec((tm, LANES), lambda b, i: (row(b, i), 0)),
                  pl.BlockSpec((1, LANES), lambda b, i: (0, 0)),
                  pl.BlockSpec((1, LANES), lambda b, i: (0, 0))],
        out_specs=[pl.BlockSpec((1, ATTN_HEADS, LANES, tm), lambda b, i: (b, 0, 0, i)),
                   pl.BlockSpec((1, aw // LANES, tm, LANES), lambda b, i: (b, 0, i, 0)),
                   pl.BlockSpec((1, ATTN_HEADS, V_ROWS, tm), lambda b, i: (b, 0, 0, i)),
                   pl.BlockSpec((1, iqw, tm), lambda b, i: (b, 0, i)),
                   pl.BlockSpec((1, tm, IDX_DIM), lambda b, i: (b, i, 0)),
                   pl.BlockSpec((1, IDX_HEADS, tm), lambda b, i: (b, 0, i))],
        out_shape=[jax.ShapeDtypeStruct((bsz, ATTN_HEADS, LANES, seq), BF16),
                   jax.ShapeDtypeStruct((bsz, aw // LANES, seq, LANES), BF16),
                   jax.ShapeDtypeStruct((bsz, ATTN_HEADS, V_ROWS, seq), BF16),
                   jax.ShapeDtypeStruct((bsz, iqw, seq), BF16),
                   jax.ShapeDtypeStruct((bsz, seq, IDX_DIM), BF16),
                   jax.ShapeDtypeStruct((bsz, IDX_HEADS, seq), F32)],
        compiler_params=_cparams(2),
    )(x2, g, sc, sh, w, cos64, sin64, qg, kg)


def _select_kernel(iq_ref, ik_ref, iw_ref, mask_ref, key_ref, *, tq, ck, n_keep, seq):
    i = pl.program_id(1)
    n_ch = ((i + 1) * tq + ck - 1) // ck
    w = iw_ref[0]
    q_pos = i * tq + lax.broadcasted_iota(jnp.int32, (ck, tq), 1)
    k_iota = lax.broadcasted_iota(jnp.int32, (ck, tq), 0)

    def score_chunk(c, carry):
        off = pl.multiple_of(c * ck, ck)
        kt = ik_ref[0, pl.ds(off, ck), :]
        acc = jnp.zeros((ck, tq), F32)
        for h in range(IDX_HEADS):
            logit = _dot(kt, iq_ref[0, h * IDX_DIM:(h + 1) * IDX_DIM, :])
            acc = acc + w[h:h + 1, :] * jnp.maximum(logit, 0.0)
        score = acc * IDX_SCALE
        key = _float_bits_to_key(pltpu.bitcast(score, jnp.int32))
        key_ref[pl.ds(off, ck), :] = jnp.where(k_iota + off <= q_pos, key, KEY_NEG_INF)
        return carry

    lax.fori_loop(0, n_ch, score_chunk, 0)

    cnt_rows = 4 * SUBLANES

    def count_ge(cand):
        def body(c, acc):
            off = pl.multiple_of(c * ck, ck)
            hit = jnp.where(key_ref[pl.ds(off, ck), :] >= cand, 1.0, 0.0)
            return acc + jnp.sum(hit.reshape(ck // cnt_rows, cnt_rows, tq), axis=0)
        acc = lax.fori_loop(0, n_ch, body, jnp.zeros((cnt_rows, tq), F32))
        return jnp.sum(acc, axis=0, keepdims=True)

    kf = jnp.float32(n_keep)

    def bit_step(it, state):
        t_u, c_ge, c_gt = state
        bit = lax.shift_left(jnp.int32(1), 31 - it)
        cand_u = t_u | bit
        cnt = count_ge(cand_u ^ INT_MIN)
        keep = cnt >= kf
        return jnp.where(keep, cand_u, t_u), jnp.where(keep, cnt, c_ge), jnp.where(keep, c_gt, cnt)

    n_all = (n_ch * ck).astype(F32) + jnp.zeros((1, tq), F32)
    t_u, cnt_ge, cnt_gt = lax.fori_loop(0, 32, bit_step, (jnp.zeros((1, tq), jnp.int32), n_all,
                                                          jnp.zeros((1, tq), F32)))
    thr = t_u ^ INT_MIN
    need = kf - cnt_gt
    tie = jnp.where((cnt_ge > kf) & (thr > KEY_NEG_INF), 1, 0)
    any_tie = jnp.max(tie) > 0
    thr_eff = jnp.maximum(thr, KEY_NEG_INF + 1)

    @pl.when(jnp.logical_not(any_tie))
    def _():
        def body(c, carry):
            off = pl.multiple_of(c * ck, ck)
            kc = key_ref[pl.ds(off, ck), :]
            mask_ref[0, pl.ds(off, ck), :] = jnp.where(kc >= thr_eff, 1, 0).astype(jnp.int8)
            return carry
        lax.fori_loop(0, n_ch, body, 0)

    @pl.when(any_tie)
    def _():
        lower = jnp.where(lax.broadcasted_iota(jnp.int32, (ck, ck), 1)
                          < lax.broadcasted_iota(jnp.int32, (ck, ck), 0), 1.0, 0.0).astype(BF16)

        def body(c, seen):
            off = pl.multiple_of(c * ck, ck)
            kc = key_ref[pl.ds(off, ck), :]
            eq = jnp.where(kc == thr, 1.0, 0.0)
            rank = seen + _dot(lower, eq.astype(BF16))
            take_eq = jnp.where(rank < need, eq, 0.0)
            sel = jnp.where(kc > thr, 1.0, take_eq)
            sel = jnp.where(kc > KEY_NEG_INF, sel, 0.0)
            mask_ref[0, pl.ds(off, ck), :] = sel.astype(jnp.int32).astype(jnp.int8)
            return seen + jnp.sum(eq, axis=0, keepdims=True)
        lax.fori_loop(0, n_ch, body, jnp.zeros((1, tq), F32))

    def zero_tail(c, carry):
        off = pl.multiple_of(c * ck, ck)
        mask_ref[0, pl.ds(off, ck), :] = jnp.zeros((ck, tq), jnp.int8)
        return carry
    lax.fori_loop(n_ch, seq // ck, zero_tail, 0)


def _select(iq_t, ik, iw_t, bsz, seq, n_keep, tq=256, ck=512):
    ck = min(ck, seq)
    return pl.pallas_call(
        functools.partial(_select_kernel, tq=tq, ck=ck, n_keep=n_keep, seq=seq),
        grid=(bsz, seq // tq),
        in_specs=[pl.BlockSpec((1, IDX_HEADS * IDX_DIM, tq), lambda b, i: (b, 0, i)),
                  pl.BlockSpec((1, seq, IDX_DIM), lambda b, i: (b, 0, 0)),
                  pl.BlockSpec((1, IDX_HEADS, tq), lambda b, i: (b, 0, i))],
        out_specs=pl.BlockSpec((1, seq, tq), lambda b, i: (b, 0, i)),
        out_shape=jax.ShapeDtypeStruct((bsz, seq, seq), jnp.int8),
        scratch_shapes=[pltpu.VMEM((seq, tq), jnp.int32)],
        compiler_params=_cparams(2),
    )(iq_t, ik, iw_t)


def _attn_kernel(qi_ref, kj_ref, q_ref, k_ref, v_ref, mask_ref, o_ref, m_ref, acc_ref, out_ref, bias_ref,
                 s0_ref, s1_ref, *, tq, tk):
    t = pl.program_id(1)
    i = qi_ref[t]
    j = kj_ref[t]
    last_j = ((i + 1) * tq - 1) // tk

    @pl.when(j == 0)
    def _():
        m_ref[...] = jnp.full(m_ref.shape, NEG_BIG, F32)
        acc_ref[...] = jnp.zeros(acc_ref.shape, F32)

    bias_ref[...] = jnp.where(mask_ref[0].astype(jnp.int32) != 0, 0.0, NEG_BIG)

    def scores(h, s_ref):
        qh = q_ref[0, h]
        for part in range(S_PARTS):
            ks = slice(part * (tk // S_PARTS), (part + 1) * (tk // S_PARTS))
            s_ref[ks, :] = _dot(k_ref[0, h // 2, ks, :], qh) + bias_ref[ks, :]

    def softmax_pv(h, s_ref):
        s = s_ref[...]
        m_old = m_ref[h:h + 1, :]
        m_new = jnp.maximum(m_old, jnp.max(s, axis=0, keepdims=True))
        alpha = jnp.exp2(m_old - m_new)
        p = jnp.exp2(s - m_new).astype(BF16)
        acc_ref[h] = alpha * acc_ref[h] + _dot(v_ref[0, h], p)
        m_ref[h:h + 1, :] = m_new

    bufs = (s0_ref, s1_ref)
    scores(0, bufs[0])
    for h in range(ATTN_HEADS):
        if h + 1 < ATTN_HEADS:
            scores(h + 1, bufs[(h + 1) % 2])
        softmax_pv(h, bufs[h % 2])

    @pl.when(j == last_j)
    def _():
        for h in range(ATTN_HEADS):
            rows = slice(h * HEAD_DIM, (h + 1) * HEAD_DIM)
            out_ref[rows, :] = acc_ref[h, 0:HEAD_DIM, :] / acc_ref[h, HEAD_DIM:HEAD_DIM + 1, :]
        o_ref[0] = out_ref[...].T.astype(o_ref.dtype)


def _attention(q_t, k_hm, v_t, mask_t, bsz, seq, tq=512, tk=512):
    tk = min(tk, seq)
    tq = min(tq, tk)
    pairs = [(i, j) for i in range(seq // tq) for j in range(((i + 1) * tq - 1) // tk + 1)]
    qi = jnp.asarray([p[0] for p in pairs], jnp.int32)
    kj = jnp.asarray([p[1] for p in pairs], jnp.int32)
    grid_spec = pltpu.PrefetchScalarGridSpec(
        num_scalar_prefetch=2,
        grid=(bsz, len(pairs)),
        in_specs=[pl.BlockSpec((1, ATTN_HEADS, LANES, tq), lambda b, t, qi, kj: (b, 0, 0, qi[t])),
                  pl.BlockSpec((1, ATTN_WIDTH // LANES, tk, LANES), lambda b, t, qi, kj: (b, 0, kj[t], 0)),
                  pl.BlockSpec((1, ATTN_HEADS, V_ROWS, tk), lambda b, t, qi, kj: (b, 0, 0, kj[t])),
                  pl.BlockSpec((1, tk, tq), lambda b, t, qi, kj: (b, kj[t], qi[t]))],
        out_specs=pl.BlockSpec((1, tq, ATTN_WIDTH), lambda b, t, qi, kj: (b, qi[t], 0)),
        scratch_shapes=[pltpu.VMEM((ATTN_HEADS, tq), F32),
                        pltpu.VMEM((ATTN_HEADS, V_ROWS, tq), F32),
                        pltpu.VMEM((ATTN_WIDTH, tq), F32),
                        pltpu.VMEM((tk, tq), F32),
                        pltpu.VMEM((tk, tq), F32),
                        pltpu.VMEM((tk, tq), F32)])
    return pl.pallas_call(
        functools.partial(_attn_kernel, tq=tq, tk=tk),
        grid_spec=grid_spec,
        out_shape=jax.ShapeDtypeStruct((bsz, seq, ATTN_WIDTH), BF16),
        compiler_params=_cparams(2),
    )(qi, kj, q_t, k_hm, v_t, mask_t)


def _ssd_kernel(z_ref, xs_ref, bc_ref, dt_ref, cwx_ref, cbx_ref, cwb_ref, cbb_ref, dtb_ref, alog_ref,
                dskip_ref, gain_ref, expand_ref, o_ref, extx_ref, extb_ref, state_ref, y_ref, *, t):
    c = pl.program_id(1)
    halo = SUBLANES

    @pl.when(c == 0)
    def _():
        extx_ref[0:halo, :] = jnp.zeros((halo, D_INNER), F32)
        extb_ref[0:halo, :] = jnp.zeros((halo, BC_WIDTH), F32)
        state_ref[...] = jnp.zeros(state_ref.shape, F32)

    def conv_silu(ext_ref, raw_ref, w_ref, b_ref):
        ext_ref[halo:halo + t, :] = raw_ref[...]
        y = b_ref[...] + w_ref[SSM_CONV - 1:SSM_CONV, :] * ext_ref[halo:halo + t, :]
        for k in range(SSM_CONV - 1):
            shift = SSM_CONV - 1 - k
            y = y + w_ref[k:k + 1, :] * ext_ref[halo - shift:halo - shift + t, :]
        ext_ref[0:halo, :] = ext_ref[t:t + halo, :]
        return _silu(y)

    xs = conv_silu(extx_ref, xs_ref, cwx_ref, cbx_ref)
    bc = conv_silu(extb_ref, bc_ref, cwb_ref, cbb_ref)

    dtv = dt_ref[...] + dtb_ref[...]
    dt = jnp.maximum(dtv, 0.0) + jnp.log1p(jnp.exp(-jnp.abs(dtv)))
    a = dt * (-jnp.exp(alog_ref[...]))

    r_io = lax.broadcasted_iota(jnp.int32, (t, t), 0)
    c_io = lax.broadcasted_iota(jnp.int32, (t, t), 1)
    tri = r_io >= c_io
    upper_incl = jnp.where(r_io <= c_io, 1.0, 0.0).astype(BF16)
    lower_incl = jnp.where(tri, 1.0, 0.0).astype(BF16)
    a_cs = _dot3_left(lower_incl, a)
    a_cs_t = _dot3(a.T, upper_incl)
    expand = expand_ref[...]
    dt_x = _dot3(dt, expand)
    acs_x = _dot3(a_cs, expand)
    alast_x = acs_x[t - 1:t, :]
    xdt = xs * dt_x
    e_start = jnp.exp(acs_x)
    xw = (xdt * jnp.exp(alast_x - acs_x)).astype(BF16)
    xdt_b = xdt.astype(BF16)
    chunk_decay = jnp.exp(alast_x)
    left_half = lax.broadcasted_iota(jnp.int32, (t, LANES), 1) < SSM_HEAD_DIM

    gw = SSM_GROUPS * D_STATE
    for g in range(SSM_GROUPS):
        bm = bc[:, g * D_STATE:(g + 1) * D_STATE]
        cm = bc[:, gw + g * D_STATE:gw + (g + 1) * D_STATE].astype(BF16)
        bm_t = bm.T.astype(BF16)
        cb = _dot_nt(cm, bm.astype(BF16))
        for p in range(HEADS_PER_GROUP // 2):
            pair = g * (HEADS_PER_GROUP // 2) + p
            lanes = slice(pair * LANES, (pair + 1) * LANES)
            xp = xdt_b[:, lanes]
            ys = []
            for e in range(2):
                h = 2 * pair + e
                seg = a_cs[:, h:h + 1] - a_cs_t[h:h + 1, :]
                dec = jnp.exp(jnp.where(tri, seg, -jnp.inf))
                ys.append(_dot((cb * dec).astype(BF16), xp))
            y_diag = jnp.where(left_half, ys[0], ys[1])
            st = state_ref[pair]
            y_off = _dot(cm, st.astype(BF16)) * e_start[:, lanes]
            y_ref[:, lanes] = y_diag + y_off
            state_ref[pair] = st * chunk_decay[:, lanes] + _dot(bm_t, xw[:, lanes])

    y = (y_ref[...] + xs * dskip_ref[...]) * _silu(z_ref[...])
    gwid = D_INNER // SSM_GROUPS
    for g in range(SSM_GROUPS):
        yg = y[:, g * gwid:(g + 1) * gwid]
        ms = jnp.mean(yg * yg, axis=-1, keepdims=True)
        o_ref[0, :, g * gwid:(g + 1) * gwid] = (yg * lax.rsqrt(ms + EPS)
                                                 * gain_ref[:, g * gwid:(g + 1) * gwid]).astype(o_ref.dtype)


def _ssd(b_proj, cwx, cbx, cwb, cbb, dtb, alog, dskip_x, gain, expand, bsz, seq):
    t = SSM_CHUNK
    nc = seq // t
    row = lambda b, c: b * nc + c
    const = lambda shape: pl.BlockSpec(shape, lambda b, c: (0,) * len(shape))
    return pl.pallas_call(
        functools.partial(_ssd_kernel, t=t),
        grid=(bsz, nc),
        in_specs=[pl.BlockSpec((t, D_INNER), lambda b, c: (row(b, c), 0)),
                  pl.BlockSpec((t, D_INNER), lambda b, c: (row(b, c), 1)),
                  pl.BlockSpec((t, BC_WIDTH), lambda b, c: (row(b, c), 2 * D_INNER // BC_WIDTH)),
                  pl.BlockSpec((t, LANES), lambda b, c: (row(b, c), (2 * D_INNER + BC_WIDTH) // LANES)),
                  const((SSM_CONV, D_INNER)), const((1, D_INNER)),
                  const((SSM_CONV, BC_WIDTH)), const((1, BC_WIDTH)),
                  const((1, LANES)), const((1, LANES)),
                  const((1, D_INNER)), const((1, D_INNER)),
                  const((LANES, D_INNER))],
        out_specs=pl.BlockSpec((1, t, D_INNER), lambda b, c: (b, c, 0)),
        out_shape=jax.ShapeDtypeStruct((bsz, seq, D_INNER), BF16),
        scratch_shapes=[pltpu.VMEM((t + SUBLANES, D_INNER), F32),
                        pltpu.VMEM((t + SUBLANES, BC_WIDTH), F32),
                        pltpu.VMEM((SSM_HEADS // 2, D_STATE, LANES), F32),
                        pltpu.VMEM((t, D_INNER), F32)],
        compiler_params=_cparams(2),
    )(b_proj, b_proj, b_proj, b_proj, cwx, cbx, cwb, cbb, dtb, alog, dskip_x, gain, expand)


def _merge_kernel(x_ref, oa_ref, ys_ref, ga_ref, gm_ref, g1_ref, wa_ref, ws_ref, wo_ref, o_ref):
    o_attn = _dot(oa_ref[...], wa_ref[...])
    o_ssm = _dot(ys_ref[...], ws_ref[...])
    mix = _sigmoid(ga_ref[...]) * o_attn + _sigmoid(gm_ref[...]) * o_ssm
    o_ref[...] = x_ref[...] + g1_ref[0] * _dot(mix.astype(BF16), wo_ref[...])


def _merge(x2, oa2, ys2, c_proj, g1, wa, ws, wo, seq, tm=256):
    m, d = x2.shape
    tpb = seq // tm
    const = lambda shape: pl.BlockSpec(shape, lambda i: (0,) * len(shape))
    return pl.pallas_call(
        _merge_kernel,
        grid=(m // tm,),
        in_specs=[pl.BlockSpec((tm, d), lambda i: (i, 0)),
                  pl.BlockSpec((tm, ATTN_WIDTH), lambda i: (i, 0)),
                  pl.BlockSpec((tm, D_INNER), lambda i: (i, 0)),
                  pl.BlockSpec((tm, d), lambda i: (i, 0)),
                  pl.BlockSpec((tm, d), lambda i: (i, 1)),
                  pl.BlockSpec((1, 1, d), lambda i: (i // tpb, 0, 0)),
                  const((ATTN_WIDTH, d)), const((D_INNER, d)), const((d, d))],
        out_specs=pl.BlockSpec((tm, d), lambda i: (i, 0)),
        out_shape=jax.ShapeDtypeStruct((m, d), F32),
        compiler_params=_cparams(1),
    )(x2, oa2, ys2, c_proj, c_proj, g1, wa, ws, wo)


def _ffn_kernel(x_ref, g_ref, sc_ref, sh_ref, g2_ref, wup_ref, cw_ref, cb_ref, wdn_ref, o_ref, ext_ref, *, tm):
    i = pl.program_id(1)
    halo = SUBLANES

    @pl.when(i == 0)
    def _():
        ext_ref[0:halo, :] = jnp.zeros((halo, 2 * D_FF), F32)

    x = x_ref[...]
    h = _norm_mod(x, g_ref[...], sc_ref[0], sh_ref[0]).astype(BF16)
    for (c0, c1) in _col_chunks(2 * D_FF, 1408):
        ext_ref[halo:halo + tm, c0:c1] = _dot(h, wup_ref[:, c0:c1])

    def conv(c0, c1):
        u = cb_ref[:, c0:c1] + cw_ref[FFN_CONV - 1:FFN_CONV, c0:c1] * ext_ref[halo:halo + tm, c0:c1]
        for k in range(FFN_CONV - 1):
            shift = FFN_CONV - 1 - k
            u = u + cw_ref[k:k + 1, c0:c1] * ext_ref[halo - shift:halo - shift + tm, c0:c1]
        return u

    out = jnp.zeros((tm, D_MODEL), F32)
    for (c0, c1) in _col_chunks(D_FF, 1408):
        act = _silu(conv(D_FF + c0, D_FF + c1)) * conv(c0, c1)
        out = out + _dot(act.astype(BF16), wdn_ref[c0:c1, :])
    o_ref[...] = x + g2_ref[0] * out
    ext_ref[0:halo, :] = ext_ref[tm:tm + halo, :]


def _ffn(x2, g, sc, sh, g2, wup, cw, cb, wdn, bsz, seq, tm=256):
    m, d = x2.shape
    tpb = seq // tm
    row = lambda b, i: b * tpb + i
    const = lambda shape: pl.BlockSpec(shape, lambda b, i: (0,) * len(shape))
    return pl.pallas_call(
        functools.partial(_ffn_kernel, tm=tm),
        grid=(bsz, tpb),
        in_specs=[pl.BlockSpec((tm, d), lambda b, i: (row(b, i), 0)),
                  const((1, d)),
                  pl.BlockSpec((1, 1, d), lambda b, i: (b, 0, 0)),
                  pl.BlockSpec((1, 1, d), lambda b, i: (b, 0, 0)),
                  pl.BlockSpec((1, 1, d), lambda b, i: (b, 0, 0)),
                  const((d, 2 * D_FF)), const((FFN_CONV, 2 * D_FF)), const((1, 2 * D_FF)),
                  const((D_FF, d))],
        out_specs=pl.BlockSpec((tm, d), lambda b, i: (row(b, i), 0)),
        out_shape=jax.ShapeDtypeStruct((m, d), F32),
        scratch_shapes=[pltpu.VMEM((tm + SUBLANES, 2 * D_FF), F32)],
        compiler_params=_cparams(2),
    )(x2, g, sc, sh, g2, wup, cw, cb, wdn)


def kernel(x, c, positions, w_ada, b_ada, norm1_g, w_in, q_norm_g, k_norm_g, ssm_conv_w, ssm_conv_b, dt_bias,
           a_log, d_skip, ssm_norm_g, w_attn_o, w_ssm_o, w_out, norm2_g, w_up, ffn_conv_w, ffn_conv_b, w_down):
    bsz, seq, d = x.shape
    depth = w_ada.shape[0]
    n_keep = min(TOPK_MAX, seq // 4)
    m = bsz * seq

    inv = 1.0 / (ROPE_THETA ** (jnp.arange(0, HEAD_DIM, 2, dtype=F32) / HEAD_DIM))
    ang = positions.astype(F32)[..., None] * inv
    cos64 = jnp.concatenate([jnp.cos(ang)] * 4, axis=-1).reshape(m, LANES)
    sin64 = jnp.concatenate([-jnp.sin(ang), jnp.sin(ang)] * 2, axis=-1).reshape(m, LANES)

    sp = np.cumsum((0, ATTN_WIDTH, ATTN_WIDTH, ATTN_WIDTH, IDX_HEADS * IDX_DIM, IDX_DIM, IDX_HEADS, D_INNER,
                    D_INNER + BC_WIDTH, SSM_HEADS, D_MODEL, D_MODEL))
    col = lambda k: w_in[:, :, sp[k]:sp[k + 1]]
    zpad = lambda n: jnp.zeros((depth, d, n), w_in.dtype)
    w_a = jnp.concatenate([col(0), col(1), col(2), col(3), col(5), zpad(LANES - IDX_DIM - IDX_HEADS), col(4)],
                          axis=-1).astype(BF16)
    w_b = jnp.concatenate([col(6), col(7), col(8), zpad(LANES - SSM_HEADS)], axis=-1).astype(BF16)
    w_c = jnp.concatenate([col(9), col(10)], axis=-1).astype(BF16)
    wa_o, ws_o, wo = w_attn_o.astype(BF16), w_ssm_o.astype(BF16), w_out.astype(BF16)
    wup, wdn = w_up.astype(BF16), w_down.astype(BF16)

    lane_pad = lambda v: jnp.pad(v, ((0, 0), (0, LANES - v.shape[-1])))
    dtb = lane_pad(dt_bias.astype(F32))
    alog = lane_pad(a_log.astype(F32))
    dskip_x = jnp.repeat(d_skip.astype(F32), SSM_HEAD_DIM, axis=-1)
    expand = (jnp.arange(LANES)[:, None] == (jnp.arange(D_INNER)[None, :] // SSM_HEAD_DIM)).astype(BF16)

    c_pad = jnp.pad(c.astype(F32), ((0, SUBLANES - bsz), (0, 0)))
    mod = _adaln(c_pad, w_ada, b_ada)[:, :bsz]

    x2 = x.reshape(m, d)
    for l in range(depth):
        sh1, sc1, g1, sh2, sc2, g2 = [mod[l, :, k * d:(k + 1) * d].reshape(bsz, 1, d) for k in range(6)]
        g_n1 = norm1_g[l].reshape(1, d)
        b_proj = _nm_matmul(x2, g_n1, sc1, sh1, w_b[l], seq)
        c_proj = _nm_matmul(x2, g_n1, sc1, sh1, w_c[l], seq)

        q_t, k_hm, v_t, iq_t, ik, iw_t = _prep(x2, g_n1, sc1, sh1, w_a[l], cos64, sin64,
                                               jnp.tile(q_norm_g[l], 2).reshape(1, LANES),
                                               jnp.tile(k_norm_g[l], 2).reshape(1, LANES), bsz, seq)
        mask_t = _select(iq_t, ik, iw_t, bsz, seq, n_keep)
        o_attn = _attention(q_t, k_hm, v_t, mask_t, bsz, seq)

        y_ssm = _ssd(b_proj, ssm_conv_w[l, :, :D_INNER], ssm_conv_b[l, :D_INNER].reshape(1, D_INNER),
                     ssm_conv_w[l, :, D_INNER:], ssm_conv_b[l, D_INNER:].reshape(1, BC_WIDTH),
                     dtb[l].reshape(1, LANES), alog[l].reshape(1, LANES), dskip_x[l].reshape(1, D_INNER),
                     ssm_norm_g[l].reshape(1, D_INNER), expand, bsz, seq)

        x2 = _merge(x2, o_attn.reshape(m, ATTN_WIDTH), y_ssm.reshape(m, D_INNER), c_proj, g1,
                    wa_o[l], ws_o[l], wo[l], seq)
        x2 = _ffn(x2, norm2_g[l].reshape(1, d), sc2, sh2, g2, wup[l], ffn_conv_w[l],
                  ffn_conv_b[l].reshape(1, 2 * D_FF), wdn[l], bsz, seq)
    return x2.reshape(bsz, seq, d)
```

```python
import functools
import math

import numpy as np
import jax
import jax.numpy as jnp
from jax import lax
from jax.experimental import pallas as pl
from jax.experimental.pallas import tpu as pltpu

F32 = jnp.float32
BF16 = jnp.bfloat16

D_MODEL = 1024
ATTN_HEADS = 16
HEAD_DIM = 64
ATTN_WIDTH = ATTN_HEADS * HEAD_DIM
IDX_HEADS = 8
IDX_DIM = 64
IDX_SCALE = (IDX_HEADS ** -0.5) * (IDX_DIM ** -0.5)
TOPK_MAX = 256
ROPE_THETA = 10000.0
D_INNER = 2 * D_MODEL
SSM_HEAD_DIM = 64
SSM_HEADS = D_INNER // SSM_HEAD_DIM
SSM_GROUPS = 4
HEADS_PER_GROUP = SSM_HEADS // SSM_GROUPS
D_STATE = 128
SSM_CONV = 4
SSM_CHUNK = 256
BC_WIDTH = 2 * SSM_GROUPS * D_STATE
D_FF = 2816
FFN_CONV = 3
EPS = 1e-6

LANES = 128
SUBLANES = 8
VMEM_LIMIT = 56 * 1024 * 1024

A_WIDTH = 3 * ATTN_WIDTH + IDX_HEADS * IDX_DIM + LANES
B_WIDTH = 2 * D_INNER + BC_WIDTH + LANES
C_WIDTH = 2 * D_MODEL

NEG_BIG = -1e30
Q_SCALE = HEAD_DIM ** -0.5 * math.log2(math.e)
V_ROWS = HEAD_DIM + 16
INT_MIN = -2 ** 31
KEY_NEG_INF = -0x7F800000
S_PARTS = 4


def _float_bits_to_key(bits):
    return jnp.where(bits >= 0, bits, INT_MIN - bits)


def _cparams(n_axes):
    return pltpu.CompilerParams(dimension_semantics=("arbitrary",) * n_axes,
                                vmem_limit_bytes=VMEM_LIMIT)


def _sigmoid(x):
    return 1.0 / (1.0 + jnp.exp(-x))


def _silu(x):
    return x * _sigmoid(x)


def _dot(a, b):
    return jnp.dot(a, b, preferred_element_type=F32)


def _dot_nt(a, b):
    return lax.dot_general(a, b, (((1,), (1,)), ((), ())), preferred_element_type=F32)


def _split3(a):
    hi = a.astype(BF16)
    r1 = a - hi.astype(F32)
    mid = r1.astype(BF16)
    lo = (r1 - mid.astype(F32)).astype(BF16)
    return hi, mid, lo


def _dot3(a, b01):
    hi, mid, lo = _split3(a)
    return _dot(hi, b01) + _dot(mid, b01) + _dot(lo, b01)


def _dot3_left(b01, a):
    hi, mid, lo = _split3(a)
    return _dot(b01, hi) + _dot(b01, mid) + _dot(b01, lo)


def _adaln_kernel(c_ref, w_ref, b_ref, o_ref):
    c = c_ref[...]
    act = _silu(c).astype(BF16)
    o_ref[0] = _dot(act, w_ref[0].astype(BF16)) + b_ref[0]


def _adaln(c_pad, w_ada, b_ada):
    depth, d, n = w_ada.shape
    tn = 1536
    return pl.pallas_call(
        _adaln_kernel,
        grid=(depth, n // tn),
        in_specs=[pl.BlockSpec((SUBLANES, d), lambda l, j: (0, 0)),
                  pl.BlockSpec((1, d, tn), lambda l, j: (l, 0, j)),
                  pl.BlockSpec((1, 1, tn), lambda l, j: (l, 0, j))],
        out_specs=pl.BlockSpec((1, SUBLANES, tn), lambda l, j: (l, 0, j)),
        out_shape=jax.ShapeDtypeStruct((depth, SUBLANES, n), F32),
        compiler_params=_cparams(2),
    )(c_pad, w_ada, b_ada.reshape(depth, 1, n))


def _norm_mod(x, g, sc, sh):
    ms = jnp.mean(x * x, axis=-1, keepdims=True)
    h = x * lax.rsqrt(ms + EPS) * g
    return h * (1.0 + sc) + sh


def _nm_matmul_kernel(x_ref, g_ref, sc_ref, sh_ref, w_ref, o_ref, *, chunks):
    h = _norm_mod(x_ref[...], g_ref[...], sc_ref[0], sh_ref[0]).astype(BF16)
    for (c0, c1) in chunks:
        o_ref[:, c0:c1] = _dot(h, w_ref[:, c0:c1]).astype(o_ref.dtype)


def _col_chunks(n, step=1024):
    out, c = [], 0
    while c < n:
        out.append((c, min(c + step, n)))
        c += step
    return tuple(out)


def _nm_matmul(x2, g, sc, sh, w, seq, tm=256):
    m, d = x2.shape
    n = w.shape[1]
    tpb = seq // tm
    return pl.pallas_call(
        functools.partial(_nm_matmul_kernel, chunks=_col_chunks(n)),
        grid=(m // tm,),
        in_specs=[pl.BlockSpec((tm, d), lambda i: (i, 0)),
                  pl.BlockSpec((1, d), lambda i: (0, 0)),
                  pl.BlockSpec((1, 1, d), lambda i: (i // tpb, 0, 0)),
                  pl.BlockSpec((1, 1, d), lambda i: (i // tpb, 0, 0)),
                  pl.BlockSpec((d, n), lambda i: (0, 0))],
        out_specs=pl.BlockSpec((tm, n), lambda i: (i, 0)),
        out_shape=jax.ShapeDtypeStruct((m, n), F32),
        compiler_params=_cparams(1),
    )(x2, g, sc, sh, w)


def _prep_kernel(x_ref, g_ref, sc_ref, sh_ref, w_ref, cos_ref, sin_ref, qg_ref, kg_ref,
                 qo_ref, ko_ref, vo_ref, iqo_ref, iko_ref, iwo_ref):
    tm = x_ref.shape[0]
    half = HEAD_DIM // 2
    cos = cos_ref[...]
    sins = sin_ref[...]
    lane = lax.broadcasted_iota(jnp.int32, (tm, LANES), 1)
    first_half = (lane & (HEAD_DIM - 1)) < half
    r_io = lax.broadcasted_iota(jnp.int32, (LANES, LANES), 0)
    c_io = lax.broadcasted_iota(jnp.int32, (LANES, LANES), 1)
    same_head = jnp.where((r_io < HEAD_DIM) == (c_io < HEAD_DIM), 1.0, 0.0).astype(BF16)

    def rope(x):
        partner = jnp.where(first_half, pltpu.roll(x, LANES - half, 1), pltpu.roll(x, half, 1))
        return x * cos + partner * sins

    def head_norm(x, g):
        sq = x * x
        hi = sq.astype(BF16)
        lo = (sq - hi.astype(F32)).astype(BF16)
        ss = _dot(hi, same_head) + _dot(lo, same_head)
        return x * lax.rsqrt(ss * (1.0 / HEAD_DIM) + EPS) * g

    h = _norm_mod(x_ref[...], g_ref[...], sc_ref[0], sh_ref[0]).astype(BF16)

    def proj(c0, width):
        return _dot(h, w_ref[:, c0:c0 + width])

    qg = qg_ref[...]
    kg = kg_ref[...]
    zeros = jnp.zeros((HEAD_DIM, tm), BF16)
    ones_row = jnp.where(lax.broadcasted_iota(jnp.int32, (V_ROWS - HEAD_DIM, tm), 0) == 0, 1.0, 0.0).astype(BF16)
    wide = 2 * LANES
    k_col, v_col, iq_col = ATTN_WIDTH, 2 * ATTN_WIDTH, 3 * ATTN_WIDTH
    misc_col = iq_col + IDX_HEADS * IDX_DIM
    for t2 in range(ATTN_WIDTH // wide):
        qq = proj(t2 * wide, wide)
        kk = proj(k_col + t2 * wide, wide)
        for u in range(2):
            t = 2 * t2 + u
            lanes = slice(u * LANES, (u + 1) * LANES)
            q_t = (rope(head_norm(qq[:, lanes], qg)) * Q_SCALE).T.astype(BF16)
            qo_ref[0, 2 * t, 0:HEAD_DIM, :] = q_t[0:HEAD_DIM]
            qo_ref[0, 2 * t, HEAD_DIM:LANES, :] = zeros
            qo_ref[0, 2 * t + 1, 0:HEAD_DIM, :] = zeros
            qo_ref[0, 2 * t + 1, HEAD_DIM:LANES, :] = q_t[HEAD_DIM:LANES]
            ko_ref[0, t] = rope(head_norm(kk[:, lanes], kg)).astype(BF16)
        v_t = proj(v_col + t2 * wide, wide).T.astype(BF16)
        for u in range(wide // HEAD_DIM):
            hv = t2 * (wide // HEAD_DIM) + u
            vo_ref[0, hv, 0:HEAD_DIM, :] = v_t[u * HEAD_DIM:(u + 1) * HEAD_DIM]
            vo_ref[0, hv, HEAD_DIM:V_ROWS, :] = ones_row
    for t2 in range(IDX_HEADS * IDX_DIM // wide):
        pp = proj(iq_col + t2 * wide, wide)
        for u in range(2):
            rows = slice((2 * t2 + u) * LANES, (2 * t2 + u + 1) * LANES)
            iqo_ref[0, rows, :] = rope(pp[:, u * LANES:(u + 1) * LANES]).T.astype(BF16)
    misc = proj(misc_col, LANES)
    iko_ref[0] = rope(misc)[:, IDX_DIM:2 * IDX_DIM].astype(BF16)
    iwo_ref[0] = misc.T[0:IDX_HEADS, :]


def _prep(x2, g, sc, sh, w, cos64, sin64, qg, kg, bsz, seq, tm=256):
    tpb = seq // tm
    row = lambda b, i: b * tpb + i
    d = x2.shape[1]
    aw = ATTN_WIDTH
    iqw = IDX_HEADS * IDX_DIM
    return pl.pallas_call(
        _prep_kernel,
        grid=(bsz, tpb),
        in_specs=[pl.BlockSpec((tm, d), lambda b, i: (row(b, i), 0)),
                  pl.BlockSpec((1, d), lambda b, i: (0, 0)),
                  pl.BlockSpec((1, 1, d), lambda b, i: (b, 0, 0)),
                  pl.BlockSpec((1, 1, d), lambda b, i: (b, 0, 0)),
                  pl.BlockSpec((d, A_WIDTH), lambda b, i: (0, 0)),
                  pl.BlockSpec((tm, LANES), lambda b, i: (row(b, i), 0)),
                  pl.BlockSpec((tm, LANES), lambda b, i: (row(b, i), 0)),
                  pl.BlockSpec((1, LANES), lambda b, i: (0, 0)),
                  pl.BlockSpec((1, LANES), lambda b, i: (0, 0))],
        out_specs=[pl.BlockSpec((1, ATTN_HEADS, LANES, tm), lambda b, i: (b, 0, 0, i)),
                   pl.BlockSpec((1, aw // LANES, tm, LANES), lambda b, i: (b, 0, i, 0)),
                   pl.BlockSpec((1, ATTN_HEADS, V_ROWS, tm), lambda b, i: (b, 0, 0, i)),
                   pl.BlockSpec((1, iqw, tm), lambda b, i: (b, 0, i)),
                   pl.BlockSpec((1, tm, IDX_DIM), lambda b, i: (b, i, 0)),
                   pl.BlockSpec((1, IDX_HEADS, tm), lambda b, i: (b, 0, i))],
        out_shape=[jax.ShapeDtypeStruct((bsz, ATTN_HEADS, LANES, seq), BF16),
                   jax.ShapeDtypeStruct((bsz, aw // LANES, seq, LANES), BF16),
                   jax.ShapeDtypeStruct((bsz, ATTN_HEADS, V_ROWS, seq), BF16),
                   jax.ShapeDtypeStruct((bsz, iqw, seq), BF16),
                   jax.ShapeDtypeStruct((bsz, seq, IDX_DIM), BF16),
                   jax.ShapeDtypeStruct((bsz, IDX_HEADS, seq), F32)],
        compiler_params=_cparams(2),
    )(x2, g, sc, sh, w, cos64, sin64, qg, kg)


def _select_kernel(iq_ref, ik_ref, iw_ref, mask_ref, key_ref, *, tq, ck, n_keep, seq):
    i = pl.program_id(1)
    n_ch = ((i + 1) * tq + ck - 1) // ck
    w = iw_ref[0]
    q_pos = i * tq + lax.broadcasted_iota(jnp.int32, (ck, tq), 1)
    k_iota = lax.broadcasted_iota(jnp.int32, (ck, tq), 0)

    def score_chunk(on_diagonal, c, carry):
        off = pl.multiple_of(c * ck, ck)
        kt = ik_ref[0, pl.ds(off, ck), :]
        acc = jnp.zeros((ck, tq), F32)
        for h in range(IDX_HEADS):
            logit = _dot(kt, iq_ref[0, h * IDX_DIM:(h + 1) * IDX_DIM, :])
            acc = acc + w[h:h + 1, :] * jnp.maximum(logit, 0.0)
        score = acc * IDX_SCALE
        key = _float_bits_to_key(pltpu.bitcast(score, jnp.int32))
        if on_diagonal:
            key = jnp.where(k_iota + off <= q_pos, key, KEY_NEG_INF)
        key_ref[pl.ds(off, ck), :] = key
        return carry

    n_full = (i * tq + 1) // ck
    lax.fori_loop(0, n_full, functools.partial(score_chunk, False), 0)
    lax.fori_loop(n_full, n_ch, functools.partial(score_chunk, True), 0)

    cnt_rows = 4 * SUBLANES

    def count_ge(cand):
        def chunk(c, acc):
            off = pl.multiple_of(c * ck, ck)
            hit = (key_ref[pl.ds(off, ck), :] >= cand).astype(jnp.int32)
            return acc + jnp.sum(hit.reshape(ck // cnt_rows, cnt_rows, tq), axis=0)

        def two_chunks(cp, acc):
            return chunk(2 * cp + 1, chunk(2 * cp, acc))
        acc = lax.fori_loop(0, n_ch // 2, two_chunks, jnp.zeros((cnt_rows, tq), jnp.int32))
        acc = lax.fori_loop(2 * (n_ch // 2), n_ch, chunk, acc)
        return jnp.sum(acc.astype(F32), axis=0, keepdims=True)

    kf = jnp.float32(n_keep)

    def bit_step(it, state):
        t_u, c_ge, c_gt = state
        bit = lax.shift_left(jnp.int32(1), 31 - it)
        cand_u = t_u | bit
        cnt = count_ge(cand_u ^ INT_MIN)
        keep = cnt >= kf
        return jnp.where(keep, cand_u, t_u), jnp.where(keep, cnt, c_ge), jnp.where(keep, c_gt, cnt)

    n_all = (n_ch * ck).astype(F32) + jnp.zeros((1, tq), F32)
    t_u, cnt_ge, cnt_gt = lax.fori_loop(0, 32, bit_step, (jnp.zeros((1, tq), jnp.int32), n_all,
                                                          jnp.zeros((1, tq), F32)))
    thr = t_u ^ INT_MIN
    need = kf - cnt_gt
    tie = jnp.where((cnt_ge > kf) & (thr > KEY_NEG_INF), 1, 0)
    any_tie = jnp.max(tie) > 0
    thr_eff = jnp.maximum(thr, KEY_NEG_INF + 1)

    @pl.when(jnp.logical_not(any_tie))
    def _():
        def body(c, carry):
            off = pl.multiple_of(c * ck, ck)
            kc = key_ref[pl.ds(off, ck), :]
            mask_ref[0, pl.ds(off, ck), :] = jnp.where(kc >= thr_eff, 1, 0).astype(jnp.int8)
            return carry
        lax.fori_loop(0, n_ch, body, 0)

    @pl.when(any_tie)
    def _():
        lower = jnp.where(lax.broadcasted_iota(jnp.int32, (ck, ck), 1)
                          < lax.broadcasted_iota(jnp.int32, (ck, ck), 0), 1.0, 0.0).astype(BF16)

        def body(c, seen):
            off = pl.multiple_of(c * ck, ck)
            kc = key_ref[pl.ds(off, ck), :]
            eq = jnp.where(kc == thr, 1.0, 0.0)
            rank = seen + _dot(lower, eq.astype(BF16))
            take_eq = jnp.where(rank < need, eq, 0.0)
            sel = jnp.where(kc > thr, 1.0, take_eq)
            sel = jnp.where(kc > KEY_NEG_INF, sel, 0.0)
            mask_ref[0, pl.ds(off, ck), :] = sel.astype(jnp.int32).astype(jnp.int8)
            return seen + jnp.sum(eq, axis=0, keepdims=True)
        lax.fori_loop(0, n_ch, body, jnp.zeros((1, tq), F32))

    def zero_tail(c, carry):
        off = pl.multiple_of(c * ck, ck)
        mask_ref[0, pl.ds(off, ck), :] = jnp.zeros((ck, tq), jnp.int8)
        return carry
    lax.fori_loop(n_ch, seq // ck, zero_tail, 0)


def _select(iq_t, ik, iw_t, bsz, seq, n_keep, tq=256, ck=512):
    ck = min(ck, seq)
    return pl.pallas_call(
        functools.partial(_select_kernel, tq=tq, ck=ck, n_keep=n_keep, seq=seq),
        grid=(bsz, seq // tq),
        in_specs=[pl.BlockSpec((1, IDX_HEADS * IDX_DIM, tq), lambda b, i: (b, 0, i)),
                  pl.BlockSpec((1, seq, IDX_DIM), lambda b, i: (b, 0, 0)),
                  pl.BlockSpec((1, IDX_HEADS, tq), lambda b, i: (b, 0, i))],
        out_specs=pl.BlockSpec((1, seq, tq), lambda b, i: (b, 0, i)),
        out_shape=jax.ShapeDtypeStruct((bsz, seq, seq), jnp.int8),
        scratch_shapes=[pltpu.VMEM((seq, tq), jnp.int32)],
        compiler_params=_cparams(2),
    )(iq_t, ik, iw_t)


def _attn_kernel(qi_ref, kj_ref, q_ref, k_ref, v_ref, mask_ref, o_ref, m_ref, acc_ref, out_ref, bias_ref,
                 s0_ref, s1_ref, *, tq, tk):
    t = pl.program_id(1)
    i = qi_ref[t]
    j = kj_ref[t]
    last_j = ((i + 1) * tq - 1) // tk

    @pl.when(j == 0)
    def _():
        m_ref[...] = jnp.full(m_ref.shape, NEG_BIG, F32)
        acc_ref[...] = jnp.zeros(acc_ref.shape, F32)

    bias_ref[...] = jnp.where(mask_ref[0].astype(jnp.int32) != 0, 0.0, NEG_BIG)

    def scores(h, s_ref):
        qh = q_ref[0, h]
        for part in range(S_PARTS):
            ks = slice(part * (tk // S_PARTS), (part + 1) * (tk // S_PARTS))
            s_ref[ks, :] = _dot(k_ref[0, h // 2, ks, :], qh) + bias_ref[ks, :]

    def softmax_pv(h, s_ref):
        s = s_ref[...]
        m_old = m_ref[h:h + 1, :]
        m_new = jnp.maximum(m_old, jnp.max(s, axis=0, keepdims=True))
        alpha = jnp.exp2(m_old - m_new)
        p = jnp.exp2(s - m_new).astype(BF16)
        acc_ref[h] = alpha * acc_ref[h] + _dot(v_ref[0, h], p)
        m_ref[h:h + 1, :] = m_new

    bufs = (s0_ref, s1_ref)
    ahead = len(bufs) - 1
    for h in range(ahead):
        scores(h, bufs[h])
    for h in range(ATTN_HEADS):
        if h + ahead < ATTN_HEADS:
            scores(h + ahead, bufs[(h + ahead) % len(bufs)])
        softmax_pv(h, bufs[h % len(bufs)])

    @pl.when(j == last_j)
    def _():
        for h in range(ATTN_HEADS):
            rows = slice(h * HEAD_DIM, (h + 1) * HEAD_DIM)
            out_ref[rows, :] = acc_ref[h, 0:HEAD_DIM, :] / acc_ref[h, HEAD_DIM:HEAD_DIM + 1, :]
        o_ref[0] = out_ref[...].T.astype(o_ref.dtype)


def _attention(q_t, k_hm, v_t, mask_t, bsz, seq, tq=512, tk=512):
    tk = min(tk, seq)
    tq = min(tq, tk)
    pairs = [(i, j) for i in range(seq // tq) for j in range(((i + 1) * tq - 1) // tk + 1)]
    qi = jnp.asarray([p[0] for p in pairs], jnp.int32)
    kj = jnp.asarray([p[1] for p in pairs], jnp.int32)
    grid_spec = pltpu.PrefetchScalarGridSpec(
        num_scalar_prefetch=2,
        grid=(bsz, len(pairs)),
        in_specs=[pl.BlockSpec((1, ATTN_HEADS, LANES, tq), lambda b, t, qi, kj: (b, 0, 0, qi[t])),
                  pl.BlockSpec((1, ATTN_WIDTH // LANES, tk, LANES), lambda b, t, qi, kj: (b, 0, kj[t], 0)),
                  pl.BlockSpec((1, ATTN_HEADS, V_ROWS, tk), lambda b, t, qi, kj: (b, 0, 0, kj[t])),
                  pl.BlockSpec((1, tk, tq), lambda b, t, qi, kj: (b, kj[t], qi[t]))],
        out_specs=pl.BlockSpec((1, tq, ATTN_WIDTH), lambda b, t, qi, kj: (b, qi[t], 0)),
        scratch_shapes=[pltpu.VMEM((ATTN_HEADS, tq), F32),
                        pltpu.VMEM((ATTN_HEADS, V_ROWS, tq), F32),
                        pltpu.VMEM((ATTN_WIDTH, tq), F32),
                        pltpu.VMEM((tk, tq), F32),
                        pltpu.VMEM((tk, tq), F32),
                        pltpu.VMEM((tk, tq), F32)])
    return pl.pallas_call(
        functools.partial(_attn_kernel, tq=tq, tk=tk),
        grid_spec=grid_spec,
        out_shape=jax.ShapeDtypeStruct((bsz, seq, ATTN_WIDTH), BF16),
        compiler_params=_cparams(2),
    )(qi, kj, q_t, k_hm, v_t, mask_t)


def _ssd_kernel(x_ref, g_ref, sc_ref, sh_ref, w_ref, cwx_ref, cbx_ref, cwb_ref, cbb_ref, dtb_ref, alog_ref,
                dskip_ref, gain_ref, expand_ref, o_ref, extx_ref, extb_ref, state_ref, y_ref, *, t):
    c = pl.program_id(1)
    halo = SUBLANES
    xs_col, bc_col, dt_col = D_INNER, 2 * D_INNER, 2 * D_INNER + BC_WIDTH

    @pl.when(c == 0)
    def _():
        extx_ref[0:halo, :] = jnp.zeros((halo, D_INNER), F32)
        extb_ref[0:halo, :] = jnp.zeros((halo, BC_WIDTH), F32)
        state_ref[...] = jnp.zeros(state_ref.shape, F32)

    h_in = _norm_mod(x_ref[...], g_ref[...], sc_ref[0], sh_ref[0]).astype(BF16)

    def proj(c0, width):
        return _dot(h_in, w_ref[:, c0:c0 + width])

    def conv_silu(ext_ref, c0, w_ref_c, b_ref):
        width = ext_ref.shape[1]
        for (a0, a1) in _col_chunks(width):
            ext_ref[halo:halo + t, a0:a1] = proj(c0 + a0, a1 - a0)
        y = b_ref[...] + w_ref_c[SSM_CONV - 1:SSM_CONV, :] * ext_ref[halo:halo + t, :]
        for k in range(SSM_CONV - 1):
            shift = SSM_CONV - 1 - k
            y = y + w_ref_c[k:k + 1, :] * ext_ref[halo - shift:halo - shift + t, :]
        ext_ref[0:halo, :] = ext_ref[t:t + halo, :]
        return _silu(y)

    xs = conv_silu(extx_ref, xs_col, cwx_ref, cbx_ref)
    bc = conv_silu(extb_ref, bc_col, cwb_ref, cbb_ref)

    dtv = proj(dt_col, LANES) + dtb_ref[...]
    dt = jnp.maximum(dtv, 0.0) + jnp.log1p(jnp.exp(-jnp.abs(dtv)))
    a = dt * (-jnp.exp(alog_ref[...]))

    r_io = lax.broadcasted_iota(jnp.int32, (t, t), 0)
    c_io = lax.broadcasted_iota(jnp.int32, (t, t), 1)
    tri = r_io >= c_io
    upper_incl = jnp.where(r_io <= c_io, 1.0, 0.0).astype(BF16)
    lower_incl = jnp.where(tri, 1.0, 0.0).astype(BF16)
    a_cs = _dot3_left(lower_incl, a)
    a_cs_t = _dot3(a.T, upper_incl)
    expand = expand_ref[...]
    dt_x = _dot3(dt, expand)
    acs_x = _dot3(a_cs, expand)
    alast_x = acs_x[t - 1:t, :]
    xdt = xs * dt_x
    e_start = jnp.exp(acs_x)
    xw = (xdt * jnp.exp(alast_x - acs_x)).astype(BF16)
    xdt_b = xdt.astype(BF16)
    chunk_decay = jnp.exp(alast_x)
    left_half = lax.broadcasted_iota(jnp.int32, (t, LANES), 1) < SSM_HEAD_DIM

    gw = SSM_GROUPS * D_STATE
    for g in range(SSM_GROUPS):
        bm = bc[:, g * D_STATE:(g + 1) * D_STATE]
        cm = bc[:, gw + g * D_STATE:gw + (g + 1) * D_STATE].astype(BF16)
        bm_t = bm.T.astype(BF16)
        cb = _dot_nt(cm, bm.astype(BF16))
        for p in range(HEADS_PER_GROUP // 2):
            pair = g * (HEADS_PER_GROUP // 2) + p
            lanes = slice(pair * LANES, (pair + 1) * LANES)
            xp = xdt_b[:, lanes]
            ys = []
            for e in range(2):
                h = 2 * pair + e
                seg = a_cs[:, h:h + 1] - a_cs_t[h:h + 1, :]
                dec = jnp.exp(jnp.where(tri, seg, -jnp.inf))
                ys.append(_dot((cb * dec).astype(BF16), xp))
            y_diag = jnp.where(left_half, ys[0], ys[1])
            st = state_ref[pair]
            y_off = _dot(cm, st.astype(BF16)) * e_start[:, lanes]
            y_ref[:, lanes] = y_diag + y_off
            state_ref[pair] = st * chunk_decay[:, lanes] + _dot(bm_t, xw[:, lanes])

    y_ref[...] = y_ref[...] + xs * dskip_ref[...]
    for (a0, a1) in _col_chunks(D_INNER):
        y_ref[:, a0:a1] = y_ref[:, a0:a1] * _silu(proj(a0, a1 - a0))
    y = y_ref[...]
    gwid = D_INNER // SSM_GROUPS
    for g in range(SSM_GROUPS):
        yg = y[:, g * gwid:(g + 1) * gwid]
        ms = jnp.mean(yg * yg, axis=-1, keepdims=True)
        o_ref[0, :, g * gwid:(g + 1) * gwid] = (yg * lax.rsqrt(ms + EPS)
                                                 * gain_ref[:, g * gwid:(g + 1) * gwid]).astype(o_ref.dtype)


def _ssd(x2, g, sc, sh, w, cwx, cbx, cwb, cbb, dtb, alog, dskip_x, gain, expand, bsz, seq):
    t = SSM_CHUNK
    nc = seq // t
    d = x2.shape[1]
    row = lambda b, c: b * nc + c
    const = lambda shape: pl.BlockSpec(shape, lambda b, c: (0,) * len(shape))
    return pl.pallas_call(
        functools.partial(_ssd_kernel, t=t),
        grid=(bsz, nc),
        in_specs=[pl.BlockSpec((t, d), lambda b, c: (row(b, c), 0)),
                  const((1, d)),
                  pl.BlockSpec((1, 1, d), lambda b, c: (b, 0, 0)),
                  pl.BlockSpec((1, 1, d), lambda b, c: (b, 0, 0)),
                  const((d, B_WIDTH)),
                  const((SSM_CONV, D_INNER)), const((1, D_INNER)),
                  const((SSM_CONV, BC_WIDTH)), const((1, BC_WIDTH)),
                  const((1, LANES)), const((1, LANES)),
                  const((1, D_INNER)), const((1, D_INNER)),
                  const((LANES, D_INNER))],
        out_specs=pl.BlockSpec((1, t, D_INNER), lambda b, c: (b, c, 0)),
        out_shape=jax.ShapeDtypeStruct((bsz, seq, D_INNER), BF16),
        scratch_shapes=[pltpu.VMEM((t + SUBLANES, D_INNER), F32),
                        pltpu.VMEM((t + SUBLANES, BC_WIDTH), F32),
                        pltpu.VMEM((SSM_HEADS // 2, D_STATE, LANES), F32),
                        pltpu.VMEM((t, D_INNER), F32)],
        compiler_params=_cparams(2),
    )(x2, g, sc, sh, w, cwx, cbx, cwb, cbb, dtb, alog, dskip_x, gain, expand)


def _merge_kernel(x_ref, oa_ref, ys_ref, ga_ref, gm_ref, g1_ref, wa_ref, ws_ref, wo_ref, o_ref):
    o_attn = _dot(oa_ref[...], wa_ref[...])
    o_ssm = _dot(ys_ref[...], ws_ref[...])
    mix = _sigmoid(ga_ref[...]) * o_attn + _sigmoid(gm_ref[...]) * o_ssm
    o_ref[...] = x_ref[...] + g1_ref[0] * _dot(mix.astype(BF16), wo_ref[...])


def _merge(x2, oa2, ys2, c_proj, g1, wa, ws, wo, seq, tm=256):
    m, d = x2.shape
    tpb = seq // tm
    const = lambda shape: pl.BlockSpec(shape, lambda i: (0,) * len(shape))
    return pl.pallas_call(
        _merge_kernel,
        grid=(m // tm,),
        in_specs=[pl.BlockSpec((tm, d), lambda i: (i, 0)),
                  pl.BlockSpec((tm, ATTN_WIDTH), lambda i: (i, 0)),
                  pl.BlockSpec((tm, D_INNER), lambda i: (i, 0)),
                  pl.BlockSpec((tm, d), lambda i: (i, 0)),
                  pl.BlockSpec((tm, d), lambda i: (i, 1)),
                  pl.BlockSpec((1, 1, d), lambda i: (i // tpb, 0, 0)),
                  const((ATTN_WIDTH, d)), const((D_INNER, d)), const((d, d))],
        out_specs=pl.BlockSpec((tm, d), lambda i: (i, 0)),
        out_shape=jax.ShapeDtypeStruct((m, d), F32),
        compiler_params=_cparams(1),
    )(x2, oa2, ys2, c_proj, c_proj, g1, wa, ws, wo)


def _ffn_kernel(x_ref, g_ref, sc_ref, sh_ref, g2_ref, wup_ref, cw_ref, cb_ref, wdn_ref, o_ref, ext_ref, *, tm):
    i = pl.program_id(1)
    halo = SUBLANES

    @pl.when(i == 0)
    def _():
        ext_ref[0:halo, :] = jnp.zeros((halo, 2 * D_FF), F32)

    x = x_ref[...]
    h = _norm_mod(x, g_ref[...], sc_ref[0], sh_ref[0]).astype(BF16)
    for (c0, c1) in _col_chunks(2 * D_FF, 1408):
        ext_ref[halo:halo + tm, c0:c1] = _dot(h, wup_ref[:, c0:c1])

    def conv(c0, c1):
        u = cb_ref[:, c0:c1] + cw_ref[FFN_CONV - 1:FFN_CONV, c0:c1] * ext_ref[halo:halo + tm, c0:c1]
        for k in range(FFN_CONV - 1):
            shift = FFN_CONV - 1 - k
            u = u + cw_ref[k:k + 1, c0:c1] * ext_ref[halo - shift:halo - shift + tm, c0:c1]
        return u

    out = jnp.zeros((tm, D_MODEL), F32)
    for (c0, c1) in _col_chunks(D_FF, 1408):
        act = _silu(conv(D_FF + c0, D_FF + c1)) * conv(c0, c1)
        out = out + _dot(act.astype(BF16), wdn_ref[c0:c1, :])
    o_ref[...] = x + g2_ref[0] * out
    ext_ref[0:halo, :] = ext_ref[tm:tm + halo, :]


def _ffn(x2, g, sc, sh, g2, wup, cw, cb, wdn, bsz, seq, tm=256):
    m, d = x2.shape
    tpb = seq // tm
    row = lambda b, i: b * tpb + i
    const = lambda shape: pl.BlockSpec(shape, lambda b, i: (0,) * len(shape))
    return pl.pallas_call(
        functools.partial(_ffn_kernel, tm=tm),
        grid=(bsz, tpb),
        in_specs=[pl.BlockSpec((tm, d), lambda b, i: (row(b, i), 0)),
                  const((1, d)),
                  pl.BlockSpec((1, 1, d), lambda b, i: (b, 0, 0)),
                  pl.BlockSpec((1, 1, d), lambda b, i: (b, 0, 0)),
                  pl.BlockSpec((1, 1, d), lambda b, i: (b, 0, 0)),
                  const((d, 2 * D_FF)), const((FFN_CONV, 2 * D_FF)), const((1, 2 * D_FF)),
                  const((D_FF, d))],
        out_specs=pl.BlockSpec((tm, d), lambda b, i: (row(b, i), 0)),
        out_shape=jax.ShapeDtypeStruct((m, d), F32),
        scratch_shapes=[pltpu.VMEM((tm + SUBLANES, 2 * D_FF), F32)],
        compiler_params=_cparams(2),
    )(x2, g, sc, sh, g2, wup, cw, cb, wdn)


def kernel(x, c, positions, w_ada, b_ada, norm1_g, w_in, q_norm_g, k_norm_g, ssm_conv_w, ssm_conv_b, dt_bias,
           a_log, d_skip, ssm_norm_g, w_attn_o, w_ssm_o, w_out, norm2_g, w_up, ffn_conv_w, ffn_conv_b, w_down):
    bsz, seq, d = x.shape
    depth = w_ada.shape[0]
    n_keep = min(TOPK_MAX, seq // 4)
    m = bsz * seq

    inv = 1.0 / (ROPE_THETA ** (jnp.arange(0, HEAD_DIM, 2, dtype=F32) / HEAD_DIM))
    ang = positions.astype(F32)[..., None] * inv
    cos64 = jnp.concatenate([jnp.cos(ang)] * 4, axis=-1).reshape(m, LANES)
    sin64 = jnp.concatenate([-jnp.sin(ang), jnp.sin(ang)] * 2, axis=-1).reshape(m, LANES)

    sp = np.cumsum((0, ATTN_WIDTH, ATTN_WIDTH, ATTN_WIDTH, IDX_HEADS * IDX_DIM, IDX_DIM, IDX_HEADS, D_INNER,
                    D_INNER + BC_WIDTH, SSM_HEADS, D_MODEL, D_MODEL))
    col = lambda k: w_in[:, :, sp[k]:sp[k + 1]]
    zpad = lambda n: jnp.zeros((depth, d, n), w_in.dtype)
    w_a = jnp.concatenate([col(0), col(1), col(2), col(3), col(5), zpad(LANES - IDX_DIM - IDX_HEADS), col(4)],
                          axis=-1).astype(BF16)
    w_b = jnp.concatenate([col(6), col(7), col(8), zpad(LANES - SSM_HEADS)], axis=-1).astype(BF16)
    w_c = jnp.concatenate([col(9), col(10)], axis=-1).astype(BF16)
    wa_o, ws_o, wo = w_attn_o.astype(BF16), w_ssm_o.astype(BF16), w_out.astype(BF16)
    wup, wdn = w_up.astype(BF16), w_down.astype(BF16)

    lane_pad = lambda v: jnp.pad(v, ((0, 0), (0, LANES - v.shape[-1])))
    dtb = lane_pad(dt_bias.astype(F32))
    alog = lane_pad(a_log.astype(F32))
    dskip_x = jnp.repeat(d_skip.astype(F32), SSM_HEAD_DIM, axis=-1)
    expand = (jnp.arange(LANES)[:, None] == (jnp.arange(D_INNER)[None, :] // SSM_HEAD_DIM)).astype(BF16)

    c_pad = jnp.pad(c.astype(F32), ((0, SUBLANES - bsz), (0, 0)))
    mod = _adaln(c_pad, w_ada, b_ada)[:, :bsz]

    x2 = x.reshape(m, d)
    for l in range(depth):
        sh1, sc1, g1, sh2, sc2, g2 = [mod[l, :, k * d:(k + 1) * d].reshape(bsz, 1, d) for k in range(6)]
        g_n1 = norm1_g[l].reshape(1, d)
        c_proj = _nm_matmul(x2, g_n1, sc1, sh1, w_c[l], seq)

        q_t, k_hm, v_t, iq_t, ik, iw_t = _prep(x2, g_n1, sc1, sh1, w_a[l], cos64, sin64,
                                               jnp.tile(q_norm_g[l], 2).reshape(1, LANES),
                                               jnp.tile(k_norm_g[l], 2).reshape(1, LANES), bsz, seq)
        mask_t = _select(iq_t, ik, iw_t, bsz, seq, n_keep)
        o_attn = _attention(q_t, k_hm, v_t, mask_t, bsz, seq)

        y_ssm = _ssd(x2, g_n1, sc1, sh1, w_b[l],
                     ssm_conv_w[l, :, :D_INNER], ssm_conv_b[l, :D_INNER].reshape(1, D_INNER),
                     ssm_conv_w[l, :, D_INNER:], ssm_conv_b[l, D_INNER:].reshape(1, BC_WIDTH),
                     dtb[l].reshape(1, LANES), alog[l].reshape(1, LANES), dskip_x[l].reshape(1, D_INNER),
                     ssm_norm_g[l].reshape(1, D_INNER), expand, bsz, seq)

        x2 = _merge(x2, o_attn.reshape(m, ATTN_WIDTH), y_ssm.reshape(m, D_INNER), c_proj, g1,
                    wa_o[l], ws_o[l], wo[l], seq)
        x2 = _ffn(x2, norm2_g[l].reshape(1, d), sc2, sh2, g2, wup[l], ffn_conv_w[l],
                  ffn_conv_b[l].reshape(1, 2 * D_FF), wdn[l], bsz, seq)
    return x2.reshape(bsz, seq, d)
```

```python
import functools
import math

import numpy as np
import jax
import jax.numpy as jnp
from jax import lax
from jax.experimental import pallas as pl
from jax.experimental.pallas import tpu as pltpu

F32 = jnp.float32
BF16 = jnp.bfloat16

D_MODEL = 1024
ATTN_HEADS = 16
HEAD_DIM = 64
ATTN_WIDTH = ATTN_HEADS * HEAD_DIM
IDX_HEADS = 8
IDX_DIM = 64
IDX_SCALE = (IDX_HEADS ** -0.5) * (IDX_DIM ** -0.5)
TOPK_MAX = 256
ROPE_THETA = 10000.0
D_INNER = 2 * D_MODEL
SSM_HEAD_DIM = 64
SSM_HEADS = D_INNER // SSM_HEAD_DIM
SSM_GROUPS = 4
HEADS_PER_GROUP = SSM_HEADS // SSM_GROUPS
D_STATE = 128
SSM_CONV = 4
SSM_CHUNK = 256
BC_WIDTH = 2 * SSM_GROUPS * D_STATE
D_FF = 2816
FFN_CONV = 3
EPS = 1e-6

LANES = 128
SUBLANES = 8
VMEM_LIMIT = 56 * 1024 * 1024

A_WIDTH = 3 * ATTN_WIDTH + IDX_HEADS * IDX_DIM + LANES
B_WIDTH = 2 * D_INNER + BC_WIDTH + LANES
C_WIDTH = 2 * D_MODEL

NEG_BIG = -1e30
Q_SCALE = HEAD_DIM ** -0.5 * math.log2(math.e)
V_ROWS = HEAD_DIM + 16
INT_MIN = -2 ** 31
KEY_NEG_INF = -0x7F800000
S_PARTS = 4


def _float_bits_to_key(bits):
    return jnp.where(bits >= 0, bits, INT_MIN - bits)


def _cparams(n_axes):
    return pltpu.CompilerParams(dimension_semantics=("arbitrary",) * n_axes,
                                vmem_limit_bytes=VMEM_LIMIT)


def _sigmoid(x):
    return 1.0 / (1.0 + jnp.exp(-x))


def _silu(x):
    return x * _sigmoid(x)


def _dot(a, b):
    return jnp.dot(a, b, preferred_element_type=F32)


def _dot_nt(a, b):
    return lax.dot_general(a, b, (((1,), (1,)), ((), ())), preferred_element_type=F32)


def _split3(a):
    hi = a.astype(BF16)
    r1 = a - hi.astype(F32)
    mid = r1.astype(BF16)
    lo = (r1 - mid.astype(F32)).astype(BF16)
    return hi, mid, lo


def _dot3(a, b01):
    hi, mid, lo = _split3(a)
    return _dot(hi, b01) + _dot(mid, b01) + _dot(lo, b01)


def _dot3_left(b01, a):
    hi, mid, lo = _split3(a)
    return _dot(b01, hi) + _dot(b01, mid) + _dot(b01, lo)


def _adaln_kernel(c_ref, w_ref, b_ref, o_ref):
    c = c_ref[...]
    act = _silu(c).astype(BF16)
    o_ref[0] = _dot(act, w_ref[0].astype(BF16)) + b_ref[0]


def _adaln(c_pad, w_ada, b_ada):
    depth, d, n = w_ada.shape
    tn = 1536
    return pl.pallas_call(
        _adaln_kernel,
        grid=(depth, n // tn),
        in_specs=[pl.BlockSpec((SUBLANES, d), lambda l, j: (0, 0)),
                  pl.BlockSpec((1, d, tn), lambda l, j: (l, 0, j)),
                  pl.BlockSpec((1, 1, tn), lambda l, j: (l, 0, j))],
        out_specs=pl.BlockSpec((1, SUBLANES, tn), lambda l, j: (l, 0, j)),
        out_shape=jax.ShapeDtypeStruct((depth, SUBLANES, n), F32),
        compiler_params=_cparams(2),
    )(c_pad, w_ada, b_ada.reshape(depth, 1, n))


def _norm_mod(x, g, sc, sh):
    ms = jnp.mean(x * x, axis=-1, keepdims=True)
    h = x * lax.rsqrt(ms + EPS) * g
    return h * (1.0 + sc) + sh


def _col_chunks(n, step=1024):
    out, c = [], 0
    while c < n:
        out.append((c, min(c + step, n)))
        c += step
    return tuple(out)


def _prep_kernel(x_ref, g_ref, sc_ref, sh_ref, w_ref, cos_ref, sin_ref, qg_ref, kg_ref,
                 qo_ref, ko_ref, vo_ref, iqo_ref, iko_ref, iwo_ref):
    tm = x_ref.shape[0]
    half = HEAD_DIM // 2
    cos = cos_ref[...]
    sins = sin_ref[...]
    lane = lax.broadcasted_iota(jnp.int32, (tm, LANES), 1)
    first_half = (lane & (HEAD_DIM - 1)) < half
    r_io = lax.broadcasted_iota(jnp.int32, (LANES, LANES), 0)
    c_io = lax.broadcasted_iota(jnp.int32, (LANES, LANES), 1)
    same_head = jnp.where((r_io < HEAD_DIM) == (c_io < HEAD_DIM), 1.0, 0.0).astype(BF16)

    def rope(x):
        partner = jnp.where(first_half, pltpu.roll(x, LANES - half, 1), pltpu.roll(x, half, 1))
        return x * cos + partner * sins

    def head_norm(x, g):
        sq = x * x
        hi = sq.astype(BF16)
        lo = (sq - hi.astype(F32)).astype(BF16)
        ss = _dot(hi, same_head) + _dot(lo, same_head)
        return x * lax.rsqrt(ss * (1.0 / HEAD_DIM) + EPS) * g

    h = _norm_mod(x_ref[...], g_ref[...], sc_ref[0], sh_ref[0]).astype(BF16)

    def proj(c0, width):
        return _dot(h, w_ref[:, c0:c0 + width])

    qg = qg_ref[...]
    kg = kg_ref[...]
    zeros = jnp.zeros((HEAD_DIM, tm), BF16)
    ones_row = jnp.where(lax.broadcasted_iota(jnp.int32, (V_ROWS - HEAD_DIM, tm), 0) == 0, 1.0, 0.0).astype(BF16)
    wide = 2 * LANES
    k_col, v_col, iq_col = ATTN_WIDTH, 2 * ATTN_WIDTH, 3 * ATTN_WIDTH
    misc_col = iq_col + IDX_HEADS * IDX_DIM
    for t2 in range(ATTN_WIDTH // wide):
        qq = proj(t2 * wide, wide)
        kk = proj(k_col + t2 * wide, wide)
        for u in range(2):
            t = 2 * t2 + u
            lanes = slice(u * LANES, (u + 1) * LANES)
            q_t = (rope(head_norm(qq[:, lanes], qg)) * Q_SCALE).T.astype(BF16)
            qo_ref[0, 2 * t, 0:HEAD_DIM, :] = q_t[0:HEAD_DIM]
            qo_ref[0, 2 * t, HEAD_DIM:LANES, :] = zeros
            qo_ref[0, 2 * t + 1, 0:HEAD_DIM, :] = zeros
            qo_ref[0, 2 * t + 1, HEAD_DIM:LANES, :] = q_t[HEAD_DIM:LANES]
            ko_ref[0, t] = rope(head_norm(kk[:, lanes], kg)).astype(BF16)
        v_t = proj(v_col + t2 * wide, wide).T.astype(BF16)
        for u in range(wide // HEAD_DIM):
            hv = t2 * (wide // HEAD_DIM) + u
            vo_ref[0, hv, 0:HEAD_DIM, :] = v_t[u * HEAD_DIM:(u + 1) * HEAD_DIM]
            vo_ref[0, hv, HEAD_DIM:V_ROWS, :] = ones_row
    for t2 in range(IDX_HEADS * IDX_DIM // wide):
        pp = proj(iq_col + t2 * wide, wide)
        for u in range(2):
            rows = slice((2 * t2 + u) * LANES, (2 * t2 + u + 1) * LANES)
            iqo_ref[0, rows, :] = rope(pp[:, u * LANES:(u + 1) * LANES]).T.astype(BF16)
    misc = proj(misc_col, LANES)
    iko_ref[0] = rope(misc)[:, IDX_DIM:2 * IDX_DIM].astype(BF16)
    iwo_ref[0] = misc.T[0:IDX_HEADS, :]


def _prep(x2, g, sc, sh, w, cos64, sin64, qg, kg, bsz, seq, tm=256):
    tpb = seq // tm
    row = lambda b, i: b * tpb + i
    d = x2.shape[1]
    aw = ATTN_WIDTH
    iqw = IDX_HEADS * IDX_DIM
    return pl.pallas_call(
        _prep_kernel,
        grid=(bsz, tpb),
        in_specs=[pl.BlockSpec((tm, d), lambda b, i: (row(b, i), 0)),
                  pl.BlockSpec((1, d), lambda b, i: (0, 0)),
                  pl.BlockSpec((1, 1, d), lambda b, i: (b, 0, 0)),
                  pl.BlockSpec((1, 1, d), lambda b, i: (b, 0, 0)),
                  pl.BlockSpec((d, A_WIDTH), lambda b, i: (0, 0)),
                  pl.BlockSpec((tm, LANES), lambda b, i: (row(b, i), 0)),
                  pl.BlockSpec((tm, LANES), lambda b, i: (row(b, i), 0)),
                  pl.BlockSpec((1, LANES), lambda b, i: (0, 0)),
                  pl.BlockSpec((1, LANES), lambda b, i: (0, 0))],
        out_specs=[pl.BlockSpec((1, ATTN_HEADS, LANES, tm), lambda b, i: (b, 0, 0, i)),
                   pl.BlockSpec((1, aw // LANES, tm, LANES), lambda b, i: (b, 0, i, 0)),
                   pl.BlockSpec((1, ATTN_HEADS, V_ROWS, tm), lambda b, i: (b, 0, 0, i)),
                   pl.BlockSpec((1, iqw, tm), lambda b, i: (b, 0, i)),
                   pl.BlockSpec((1, tm, IDX_DIM), lambda b, i: (b, i, 0)),
                   pl.BlockSpec((1, IDX_HEADS, tm), lambda b, i: (b, 0, i))],
        out_shape=[jax.ShapeDtypeStruct((bsz, ATTN_HEADS, LANES, seq), BF16),
                   jax.ShapeDtypeStruct((bsz, aw // LANES, seq, LANES), BF16),
                   jax.ShapeDtypeStruct((bsz, ATTN_HEADS, V_ROWS, seq), BF16),
                   jax.ShapeDtypeStruct((bsz, iqw, seq), BF16),
                   jax.ShapeDtypeStruct((bsz, seq, IDX_DIM), BF16),
                   jax.ShapeDtypeStruct((bsz, IDX_HEADS, seq), F32)],
        compiler_params=_cparams(2),
    )(x2, g, sc, sh, w, cos64, sin64, qg, kg)


def _select_kernel(iq_ref, ik_ref, iw_ref, mask_ref, key_ref, *, tq, ck, n_keep, seq):
    i = pl.program_id(1)
    n_ch = ((i + 1) * tq + ck - 1) // ck
    w = iw_ref[0]
    q_pos = i * tq + lax.broadcasted_iota(jnp.int32, (ck, tq), 1)
    k_iota = lax.broadcasted_iota(jnp.int32, (ck, tq), 0)

    def score_chunk(on_diagonal, c, carry):
        off = pl.multiple_of(c * ck, ck)
        kt = ik_ref[0, pl.ds(off, ck), :]
        acc = jnp.zeros((ck, tq), F32)
        for h in range(IDX_HEADS):
            logit = _dot(kt, iq_ref[0, h * IDX_DIM:(h + 1) * IDX_DIM, :])
            acc = acc + w[h:h + 1, :] * jnp.maximum(logit, 0.0)
        score = acc * IDX_SCALE
        key = _float_bits_to_key(pltpu.bitcast(score, jnp.int32))
        if on_diagonal:
            key = jnp.where(k_iota + off <= q_pos, key, KEY_NEG_INF)
        key_ref[pl.ds(off, ck), :] = key
        return carry

    n_full = (i * tq + 1) // ck
    lax.fori_loop(0, n_full, functools.partial(score_chunk, False), 0)
    lax.fori_loop(n_full, n_ch, functools.partial(score_chunk, True), 0)

    cnt_rows = 4 * SUBLANES

    def count_ge(cand):
        def chunk(c, acc):
            off = pl.multiple_of(c * ck, ck)
            hit = (key_ref[pl.ds(off, ck), :] >= cand).astype(jnp.int32)
            return acc + jnp.sum(hit.reshape(ck // cnt_rows, cnt_rows, tq), axis=0)

        def two_chunks(cp, acc):
            return chunk(2 * cp + 1, chunk(2 * cp, acc))
        acc = lax.fori_loop(0, n_ch // 2, two_chunks, jnp.zeros((cnt_rows, tq), jnp.int32))
        acc = lax.fori_loop(2 * (n_ch // 2), n_ch, chunk, acc)
        return jnp.sum(acc.astype(F32), axis=0, keepdims=True)

    kf = jnp.float32(n_keep)

    def bit_step(it, state):
        t_u, c_ge, c_gt = state
        bit = lax.shift_left(jnp.int32(1), 31 - it)
        cand_u = t_u | bit
        cnt = count_ge(cand_u ^ INT_MIN)
        keep = cnt >= kf
        return jnp.where(keep, cand_u, t_u), jnp.where(keep, cnt, c_ge), jnp.where(keep, c_gt, cnt)

    n_all = (n_ch * ck).astype(F32) + jnp.zeros((1, tq), F32)
    t_u, cnt_ge, cnt_gt = lax.fori_loop(0, 32, bit_step, (jnp.zeros((1, tq), jnp.int32), n_all,
                                                          jnp.zeros((1, tq), F32)))
    thr = t_u ^ INT_MIN
    need = kf - cnt_gt
    tie = jnp.where((cnt_ge > kf) & (thr > KEY_NEG_INF), 1, 0)
    any_tie = jnp.max(tie) > 0
    thr_eff = jnp.maximum(thr, KEY_NEG_INF + 1)

    @pl.when(jnp.logical_not(any_tie))
    def _():
        def body(c, carry):
            off = pl.multiple_of(c * ck, ck)
            kc = key_ref[pl.ds(off, ck), :]
            mask_ref[0, pl.ds(off, ck), :] = jnp.where(kc >= thr_eff, 1, 0).astype(jnp.int8)
            return carry
        lax.fori_loop(0, n_ch, body, 0)

    @pl.when(any_tie)
    def _():
        lower = jnp.where(lax.broadcasted_iota(jnp.int32, (ck, ck), 1)
                          < lax.broadcasted_iota(jnp.int32, (ck, ck), 0), 1.0, 0.0).astype(BF16)

        def body(c, seen):
            off = pl.multiple_of(c * ck, ck)
            kc = key_ref[pl.ds(off, ck), :]
            eq = jnp.where(kc == thr, 1.0, 0.0)
            rank = seen + _dot(lower, eq.astype(BF16))
            take_eq = jnp.where(rank < need, eq, 0.0)
            sel = jnp.where(kc > thr, 1.0, take_eq)
            sel = jnp.where(kc > KEY_NEG_INF, sel, 0.0)
            mask_ref[0, pl.ds(off, ck), :] = sel.astype(jnp.int32).astype(jnp.int8)
            return seen + jnp.sum(eq, axis=0, keepdims=True)
        lax.fori_loop(0, n_ch, body, jnp.zeros((1, tq), F32))

    def zero_tail(c, carry):
        off = pl.multiple_of(c * ck, ck)
        mask_ref[0, pl.ds(off, ck), :] = jnp.zeros((ck, tq), jnp.int8)
        return carry
    lax.fori_loop(n_ch, seq // ck, zero_tail, 0)


def _select(iq_t, ik, iw_t, bsz, seq, n_keep, tq=256, ck=512):
    ck = min(ck, seq)
    return pl.pallas_call(
        functools.partial(_select_kernel, tq=tq, ck=ck, n_keep=n_keep, seq=seq),
        grid=(bsz, seq // tq),
        in_specs=[pl.BlockSpec((1, IDX_HEADS * IDX_DIM, tq), lambda b, i: (b, 0, i)),
                  pl.BlockSpec((1, seq, IDX_DIM), lambda b, i: (b, 0, 0)),
                  pl.BlockSpec((1, IDX_HEADS, tq), lambda b, i: (b, 0, i))],
        out_specs=pl.BlockSpec((1, seq, tq), lambda b, i: (b, 0, i)),
        out_shape=jax.ShapeDtypeStruct((bsz, seq, seq), jnp.int8),
        scratch_shapes=[pltpu.VMEM((seq, tq), jnp.int32)],
        compiler_params=_cparams(2),
    )(iq_t, ik, iw_t)


def _attn_kernel(qi_ref, kj_ref, q_ref, k_ref, v_ref, mask_ref, o_ref, m_ref, acc_ref, out_ref, bias_ref,
                 s0_ref, s1_ref, *, tq, tk):
    t = pl.program_id(1)
    i = qi_ref[t]
    j = kj_ref[t]
    last_j = ((i + 1) * tq - 1) // tk

    @pl.when(j == 0)
    def _():
        m_ref[...] = jnp.full(m_ref.shape, NEG_BIG, F32)
        acc_ref[...] = jnp.zeros(acc_ref.shape, F32)

    bias_ref[...] = jnp.where(mask_ref[0].astype(jnp.int32) != 0, 0.0, NEG_BIG)

    def scores(h, s_ref):
        qh = q_ref[0, h]
        for part in range(S_PARTS):
            ks = slice(part * (tk // S_PARTS), (part + 1) * (tk // S_PARTS))
            s_ref[ks, :] = _dot(k_ref[0, h // 2, ks, :], qh) + bias_ref[ks, :]

    def softmax_pv(h, s_ref):
        s = s_ref[...]
        m_old = m_ref[h:h + 1, :]
        m_new = jnp.maximum(m_old, jnp.max(s, axis=0, keepdims=True))
        alpha = jnp.exp2(m_old - m_new)
        p = jnp.exp2(s - m_new).astype(BF16)
        acc_ref[h] = alpha * acc_ref[h] + _dot(v_ref[0, h], p)
        m_ref[h:h + 1, :] = m_new

    bufs = (s0_ref, s1_ref)
    ahead = len(bufs) - 1
    for h in range(ahead):
        scores(h, bufs[h])
    for h in range(ATTN_HEADS):
        if h + ahead < ATTN_HEADS:
            scores(h + ahead, bufs[(h + ahead) % len(bufs)])
        softmax_pv(h, bufs[h % len(bufs)])

    @pl.when(j == last_j)
    def _():
        for h in range(ATTN_HEADS):
            rows = slice(h * HEAD_DIM, (h + 1) * HEAD_DIM)
            out_ref[rows, :] = acc_ref[h, 0:HEAD_DIM, :] / acc_ref[h, HEAD_DIM:HEAD_DIM + 1, :]
        o_ref[0] = out_ref[...].T.astype(o_ref.dtype)


def _attention(q_t, k_hm, v_t, mask_t, bsz, seq, tq=512, tk=512):
    tk = min(tk, seq)
    tq = min(tq, tk)
    pairs = [(i, j) for i in range(seq // tq) for j in range(((i + 1) * tq - 1) // tk + 1)]
    qi = jnp.asarray([p[0] for p in pairs], jnp.int32)
    kj = jnp.asarray([p[1] for p in pairs], jnp.int32)
    grid_spec = pltpu.PrefetchScalarGridSpec(
        num_scalar_prefetch=2,
        grid=(bsz, len(pairs)),
        in_specs=[pl.BlockSpec((1, ATTN_HEADS, LANES, tq), lambda b, t, qi, kj: (b, 0, 0, qi[t])),
                  pl.BlockSpec((1, ATTN_WIDTH // LANES, tk, LANES), lambda b, t, qi, kj: (b, 0, kj[t], 0)),
                  pl.BlockSpec((1, ATTN_HEADS, V_ROWS, tk), lambda b, t, qi, kj: (b, 0, 0, kj[t])),
                  pl.BlockSpec((1, tk, tq), lambda b, t, qi, kj: (b, kj[t], qi[t]))],
        out_specs=pl.BlockSpec((1, tq, ATTN_WIDTH), lambda b, t, qi, kj: (b, qi[t], 0)),
        scratch_shapes=[pltpu.VMEM((ATTN_HEADS, tq), F32),
                        pltpu.VMEM((ATTN_HEADS, V_ROWS, tq), F32),
                        pltpu.VMEM((ATTN_WIDTH, tq), F32),
                        pltpu.VMEM((tk, tq), F32),
                        pltpu.VMEM((tk, tq), F32),
                        pltpu.VMEM((tk, tq), F32)])
    return pl.pallas_call(
        functools.partial(_attn_kernel, tq=tq, tk=tk),
        grid_spec=grid_spec,
        out_shape=jax.ShapeDtypeStruct((bsz, seq, ATTN_WIDTH), BF16),
        compiler_params=_cparams(2),
    )(qi, kj, q_t, k_hm, v_t, mask_t)


def _ssd_kernel(x_ref, g_ref, sc_ref, sh_ref, w_ref, cwx_ref, cbx_ref, cwb_ref, cbb_ref, dtb_ref, alog_ref,
                dskip_ref, gain_ref, expand_ref, o_ref, extx_ref, extb_ref, state_ref, y_ref, *, t):
    c = pl.program_id(1)
    halo = SUBLANES
    xs_col, bc_col, dt_col = D_INNER, 2 * D_INNER, 2 * D_INNER + BC_WIDTH

    @pl.when(c == 0)
    def _():
        extx_ref[0:halo, :] = jnp.zeros((halo, D_INNER), F32)
        extb_ref[0:halo, :] = jnp.zeros((halo, BC_WIDTH), F32)
        state_ref[...] = jnp.zeros(state_ref.shape, F32)

    h_in = _norm_mod(x_ref[...], g_ref[...], sc_ref[0], sh_ref[0]).astype(BF16)

    def proj(c0, width):
        return _dot(h_in, w_ref[:, c0:c0 + width])

    def conv_silu(ext_ref, c0, w_ref_c, b_ref):
        width = ext_ref.shape[1]
        for (a0, a1) in _col_chunks(width):
            ext_ref[halo:halo + t, a0:a1] = proj(c0 + a0, a1 - a0)
        y = b_ref[...] + w_ref_c[SSM_CONV - 1:SSM_CONV, :] * ext_ref[halo:halo + t, :]
        for k in range(SSM_CONV - 1):
            shift = SSM_CONV - 1 - k
            y = y + w_ref_c[k:k + 1, :] * ext_ref[halo - shift:halo - shift + t, :]
        ext_ref[0:halo, :] = ext_ref[t:t + halo, :]
        return _silu(y)

    xs = conv_silu(extx_ref, xs_col, cwx_ref, cbx_ref)
    bc = conv_silu(extb_ref, bc_col, cwb_ref, cbb_ref)

    dtv = proj(dt_col, LANES) + dtb_ref[...]
    dt = jnp.maximum(dtv, 0.0) + jnp.log1p(jnp.exp(-jnp.abs(dtv)))
    a = dt * (-jnp.exp(alog_ref[...]))

    r_io = lax.broadcasted_iota(jnp.int32, (t, t), 0)
    c_io = lax.broadcasted_iota(jnp.int32, (t, t), 1)
    tri = r_io >= c_io
    upper_incl = jnp.where(r_io <= c_io, 1.0, 0.0).astype(BF16)
    lower_incl = jnp.where(tri, 1.0, 0.0).astype(BF16)
    a_cs = _dot3_left(lower_incl, a)
    a_cs_t = _dot3(a.T, upper_incl)
    expand = expand_ref[...]
    dt_x = _dot3(dt, expand)
    acs_x = _dot3(a_cs, expand)
    alast_x = acs_x[t - 1:t, :]
    xdt = xs * dt_x
    e_start = jnp.exp(acs_x)
    xw = (xdt * jnp.exp(alast_x - acs_x)).astype(BF16)
    xdt_b = xdt.astype(BF16)
    chunk_decay = jnp.exp(alast_x)
    left_half = lax.broadcasted_iota(jnp.int32, (t, LANES), 1) < SSM_HEAD_DIM

    gw = SSM_GROUPS * D_STATE
    for g in range(SSM_GROUPS):
        bm = bc[:, g * D_STATE:(g + 1) * D_STATE]
        cm = bc[:, gw + g * D_STATE:gw + (g + 1) * D_STATE].astype(BF16)
        bm_t = bm.T.astype(BF16)
        cb = _dot_nt(cm, bm.astype(BF16))
        for p in range(HEADS_PER_GROUP // 2):
            pair = g * (HEADS_PER_GROUP // 2) + p
            lanes = slice(pair * LANES, (pair + 1) * LANES)
            xp = xdt_b[:, lanes]
            ys = []
            for e in range(2):
                h = 2 * pair + e
                seg = a_cs[:, h:h + 1] - a_cs_t[h:h + 1, :]
                dec = jnp.exp(jnp.where(tri, seg, -jnp.inf))
                ys.append(_dot((cb * dec).astype(BF16), xp))
            y_diag = jnp.where(left_half, ys[0], ys[1])
            st = state_ref[pair]
            y_off = _dot(cm, st.astype(BF16)) * e_start[:, lanes]
            y_ref[:, lanes] = y_diag + y_off
            state_ref[pair] = st * chunk_decay[:, lanes] + _dot(bm_t, xw[:, lanes])

    y_ref[...] = y_ref[...] + xs * dskip_ref[...]
    for (a0, a1) in _col_chunks(D_INNER):
        y_ref[:, a0:a1] = y_ref[:, a0:a1] * _silu(proj(a0, a1 - a0))
    y = y_ref[...]
    gwid = D_INNER // SSM_GROUPS
    for g in range(SSM_GROUPS):
        yg = y[:, g * gwid:(g + 1) * gwid]
        ms = jnp.mean(yg * yg, axis=-1, keepdims=True)
        o_ref[0, :, g * gwid:(g + 1) * gwid] = (yg * lax.rsqrt(ms + EPS)
                                                 * gain_ref[:, g * gwid:(g + 1) * gwid]).astype(o_ref.dtype)


def _ssd(x2, g, sc, sh, w, cwx, cbx, cwb, cbb, dtb, alog, dskip_x, gain, expand, bsz, seq):
    t = SSM_CHUNK
    nc = seq // t
    d = x2.shape[1]
    row = lambda b, c: b * nc + c
    const = lambda shape: pl.BlockSpec(shape, lambda b, c: (0,) * len(shape))
    return pl.pallas_call(
        functools.partial(_ssd_kernel, t=t),
        grid=(bsz, nc),
        in_specs=[pl.BlockSpec((t, d), lambda b, c: (row(b, c), 0)),
                  const((1, d)),
                  pl.BlockSpec((1, 1, d), lambda b, c: (b, 0, 0)),
                  pl.BlockSpec((1, 1, d), lambda b, c: (b, 0, 0)),
                  const((d, B_WIDTH)),
                  const((SSM_CONV, D_INNER)), const((1, D_INNER)),
                  const((SSM_CONV, BC_WIDTH)), const((1, BC_WIDTH)),
                  const((1, LANES)), const((1, LANES)),
                  const((1, D_INNER)), const((1, D_INNER)),
                  const((LANES, D_INNER))],
        out_specs=pl.BlockSpec((1, t, D_INNER), lambda b, c: (b, c, 0)),
        out_shape=jax.ShapeDtypeStruct((bsz, seq, D_INNER), BF16),
        scratch_shapes=[pltpu.VMEM((t + SUBLANES, D_INNER), F32),
                        pltpu.VMEM((t + SUBLANES, BC_WIDTH), F32),
                        pltpu.VMEM((SSM_HEADS // 2, D_STATE, LANES), F32),
                        pltpu.VMEM((t, D_INNER), F32)],
        compiler_params=_cparams(2),
    )(x2, g, sc, sh, w, cwx, cbx, cwb, cbb, dtb, alog, dskip_x, gain, expand)


def _merge_kernel(x_ref, oa_ref, ys_ref, g_ref, sc_ref, sh_ref, g1_ref, wc_ref, wa_ref, ws_ref, wo_ref, o_ref):
    x = x_ref[...]
    d = x.shape[1]
    h = _norm_mod(x, g_ref[...], sc_ref[0], sh_ref[0]).astype(BF16)
    mix = _sigmoid(_dot(h, wc_ref[:, 0:d])) * _dot(oa_ref[...], wa_ref[...])
    mix = mix + _sigmoid(_dot(h, wc_ref[:, d:2 * d])) * _dot(ys_ref[...], ws_ref[...])
    o_ref[...] = x + g1_ref[0] * _dot(mix.astype(BF16), wo_ref[...])


def _merge(x2, oa2, ys2, g, sc, sh, g1, wc, wa, ws, wo, seq, tm=256):
    m, d = x2.shape
    tpb = seq // tm
    const = lambda shape: pl.BlockSpec(shape, lambda i: (0,) * len(shape))
    per_batch = pl.BlockSpec((1, 1, d), lambda i: (i // tpb, 0, 0))
    return pl.pallas_call(
        _merge_kernel,
        grid=(m // tm,),
        in_specs=[pl.BlockSpec((tm, d), lambda i: (i, 0)),
                  pl.BlockSpec((tm, ATTN_WIDTH), lambda i: (i, 0)),
                  pl.BlockSpec((tm, D_INNER), lambda i: (i, 0)),
                  const((1, d)), per_batch, per_batch, per_batch,
                  const((d, C_WIDTH)), const((ATTN_WIDTH, d)), const((D_INNER, d)), const((d, d))],
        out_specs=pl.BlockSpec((tm, d), lambda i: (i, 0)),
        out_shape=jax.ShapeDtypeStruct((m, d), F32),
        compiler_params=_cparams(1),
    )(x2, oa2, ys2, g, sc, sh, g1, wc, wa, ws, wo)


def _ffn_kernel(x_ref, g_ref, sc_ref, sh_ref, g2_ref, wup_ref, cw_ref, cb_ref, wdn_ref, o_ref, ext_ref, *, tm):
    i = pl.program_id(1)
    halo = SUBLANES

    @pl.when(i == 0)
    def _():
        ext_ref[0:halo, :] = jnp.zeros((halo, 2 * D_FF), F32)

    x = x_ref[...]
    h = _norm_mod(x, g_ref[...], sc_ref[0], sh_ref[0]).astype(BF16)
    for (c0, c1) in _col_chunks(2 * D_FF, 1408):
        ext_ref[halo:halo + tm, c0:c1] = _dot(h, wup_ref[:, c0:c1])

    def conv(c0, c1):
        u = cb_ref[:, c0:c1] + cw_ref[FFN_CONV - 1:FFN_CONV, c0:c1] * ext_ref[halo:halo + tm, c0:c1]
        for k in range(FFN_CONV - 1):
            shift = FFN_CONV - 1 - k
            u = u + cw_ref[k:k + 1, c0:c1] * ext_ref[halo - shift:halo - shift + tm, c0:c1]
        return u

    out = jnp.zeros((tm, D_MODEL), F32)
    for (c0, c1) in _col_chunks(D_FF, 1408):
        act = _silu(conv(D_FF + c0, D_FF + c1)) * conv(c0, c1)
        out = out + _dot(act.astype(BF16), wdn_ref[c0:c1, :])
    o_ref[...] = x + g2_ref[0] * out
    ext_ref[0:halo, :] = ext_ref[tm:tm + halo, :]


def _ffn(x2, g, sc, sh, g2, wup, cw, cb, wdn, bsz, seq, tm=256):
    m, d = x2.shape
    tpb = seq // tm
    row = lambda b, i: b * tpb + i
    const = lambda shape: pl.BlockSpec(shape, lambda b, i: (0,) * len(shape))
    return pl.pallas_call(
        functools.partial(_ffn_kernel, tm=tm),
        grid=(bsz, tpb),
        in_specs=[pl.BlockSpec((tm, d), lambda b, i: (row(b, i), 0)),
                  const((1, d)),
                  pl.BlockSpec((1, 1, d), lambda b, i: (b, 0, 0)),
                  pl.BlockSpec((1, 1, d), lambda b, i: (b, 0, 0)),
                  pl.BlockSpec((1, 1, d), lambda b, i: (b, 0, 0)),
                  const((d, 2 * D_FF)), const((FFN_CONV, 2 * D_FF)), const((1, 2 * D_FF)),
                  const((D_FF, d))],
        out_specs=pl.BlockSpec((tm, d), lambda b, i: (row(b, i), 0)),
        out_shape=jax.ShapeDtypeStruct((m, d), F32),
        scratch_shapes=[pltpu.VMEM((tm + SUBLANES, 2 * D_FF), F32)],
        compiler_params=_cparams(2),
    )(x2, g, sc, sh, g2, wup, cw, cb, wdn)


def kernel(x, c, positions, w_ada, b_ada, norm1_g, w_in, q_norm_g, k_norm_g, ssm_conv_w, ssm_conv_b, dt_bias,
           a_log, d_skip, ssm_norm_g, w_attn_o, w_ssm_o, w_out, norm2_g, w_up, ffn_conv_w, ffn_conv_b, w_down):
    bsz, seq, d = x.shape
    depth = w_ada.shape[0]
    n_keep = min(TOPK_MAX, seq // 4)
    m = bsz * seq

    inv = 1.0 / (ROPE_THETA ** (jnp.arange(0, HEAD_DIM, 2, dtype=F32) / HEAD_DIM))
    ang = positions.astype(F32)[..., None] * inv
    cos64 = jnp.concatenate([jnp.cos(ang)] * 4, axis=-1).reshape(m, LANES)
    sin64 = jnp.concatenate([-jnp.sin(ang), jnp.sin(ang)] * 2, axis=-1).reshape(m, LANES)

    sp = np.cumsum((0, ATTN_WIDTH, ATTN_WIDTH, ATTN_WIDTH, IDX_HEADS * IDX_DIM, IDX_DIM, IDX_HEADS, D_INNER,
                    D_INNER + BC_WIDTH, SSM_HEADS, D_MODEL, D_MODEL))
    col = lambda k: w_in[:, :, sp[k]:sp[k + 1]]
    zpad = lambda n: jnp.zeros((depth, d, n), w_in.dtype)
    w_a = jnp.concatenate([col(0), col(1), col(2), col(3), col(5), zpad(LANES - IDX_DIM - IDX_HEADS), col(4)],
                          axis=-1).astype(BF16)
    w_b = jnp.concatenate([col(6), col(7), col(8), zpad(LANES - SSM_HEADS)], axis=-1).astype(BF16)
    w_c = jnp.concatenate([col(9), col(10)], axis=-1).astype(BF16)
    wa_o, ws_o, wo = w_attn_o.astype(BF16), w_ssm_o.astype(BF16), w_out.astype(BF16)
    wup, wdn = w_up.astype(BF16), w_down.astype(BF16)

    lane_pad = lambda v: jnp.pad(v, ((0, 0), (0, LANES - v.shape[-1])))
    dtb = lane_pad(dt_bias.astype(F32))
    alog = lane_pad(a_log.astype(F32))
    dskip_x = jnp.repeat(d_skip.astype(F32), SSM_HEAD_DIM, axis=-1)
    expand = (jnp.arange(LANES)[:, None] == (jnp.arange(D_INNER)[None, :] // SSM_HEAD_DIM)).astype(BF16)

    c_pad = jnp.pad(c.astype(F32), ((0, SUBLANES - bsz), (0, 0)))
    mod = _adaln(c_pad, w_ada, b_ada)[:, :bsz]

    x2 = x.reshape(m, d)
    for l in range(depth):
        sh1, sc1, g1, sh2, sc2, g2 = [mod[l, :, k * d:(k + 1) * d].reshape(bsz, 1, d) for k in range(6)]
        g_n1 = norm1_g[l].reshape(1, d)
        q_t, k_hm, v_t, iq_t, ik, iw_t = _prep(x2, g_n1, sc1, sh1, w_a[l], cos64, sin64,
                                               jnp.tile(q_norm_g[l], 2).reshape(1, LANES),
                                               jnp.tile(k_norm_g[l], 2).reshape(1, LANES), bsz, seq)
        mask_t = _select(iq_t, ik, iw_t, bsz, seq, n_keep)
        o_attn = _attention(q_t, k_hm, v_t, mask_t, bsz, seq)

        y_ssm = _ssd(x2, g_n1, sc1, sh1, w_b[l],
                     ssm_conv_w[l, :, :D_INNER], ssm_conv_b[l, :D_INNER].reshape(1, D_INNER),
                     ssm_conv_w[l, :, D_INNER:], ssm_conv_b[l, D_INNER:].reshape(1, BC_WIDTH),
                     dtb[l].reshape(1, LANES), alog[l].reshape(1, LANES), dskip_x[l].reshape(1, D_INNER),
                     ssm_norm_g[l].reshape(1, D_INNER), expand, bsz, seq)

        x2 = _merge(x2, o_attn.reshape(m, ATTN_WIDTH), y_ssm.reshape(m, D_INNER), g_n1, sc1, sh1, g1,
                    w_c[l], wa_o[l], ws_o[l], wo[l], seq)
        x2 = _ffn(x2, norm2_g[l].reshape(1, d), sc2, sh2, g2, wup[l], ffn_conv_w[l],
                  ffn_conv_b[l].reshape(1, 2 * D_FF), wdn[l], bsz, seq)
    return x2.reshape(bsz, seq, d)
```
